```python
import jax, jax.numpy as jnp
from jax import lax
import numpy as np

D_MODEL = 1024
BATCH = 16
SEQ = 2048
DEPTH = 2
DEC_BATCH = 32
DEC_SEQ = 16
PAST_LEN = 2048

CHUNK = 64
HEAD_DIM = 64
SB_WIDTH = D_MODEL // 2
N_SB_HEADS = SB_WIDTH // HEAD_DIM
CONV_CH = D_MODEL - SB_WIDTH
CONV_WIDTH = 31
CONV_STATE = CONV_WIDTH - 1
N_C_HEADS = D_MODEL // HEAD_DIM
C_WIDTH = N_C_HEADS * HEAD_DIM
LEFT_CHUNKS = 8
BAND_PAST = LEFT_CHUNKS * CHUNK
REL_CLIP = 128
N_REL = 2 * REL_CLIP + 1
D_FF = 4 * D_MODEL
Q_BLOCK = 128
N_AB = (DEPTH + 1) // 2
N_C = DEPTH // 2
AB_IN = 3 * SB_WIDTH + 2 * CONV_CH
C_IN = 3 * C_WIDTH
RMS_EPS = 1e-6
LN_EPS = 1e-5

kernel_name = 'hybrid_streaming_sb_conv_chunkband_step'


def rmsnorm(x, g):
    xf = x.astype(jnp.float32)
    r = lax.rsqrt(jnp.mean(xf * xf, axis=-1, keepdims=True) + RMS_EPS)
    return (xf * r).astype(x.dtype) * g


def layernorm(x, g, b):
    xf = x.astype(jnp.float32)
    mu = jnp.mean(xf, axis=-1, keepdims=True)
    var = jnp.mean(jnp.square(xf - mu), axis=-1, keepdims=True)
    return ((xf - mu) * lax.rsqrt(var + LN_EPS)).astype(x.dtype) * g + b


def sq_relu_mlp(h, w_up, w_down):
    return jnp.square(jax.nn.relu(h @ w_up)) @ w_down


def stick_breaking(q, k, v, q_pos, k_pos):
    z = jnp.einsum('bqhd,bkhd->bhqk', q, k).astype(jnp.float32) * (HEAD_DIM ** -0.5)
    mask = k_pos[None, :] < q_pos[:, None]
    log_keep = jnp.where(mask, jax.nn.log_sigmoid(-z), 0.0)
    between = lax.cumsum(log_keep, axis=3, reverse=True) - log_keep
    w = jnp.where(mask, jnp.exp(jax.nn.log_sigmoid(z) + between), 0.0)
    return jnp.einsum('bhqk,bkhd->bqhd', w.astype(v.dtype), v)


def band_attention(q, k, v, q_pos, k_pos, rel_bias):
    s = jnp.einsum('bqhd,bkhd->bhqk', q, k).astype(jnp.float32) * (HEAD_DIM ** -0.5)
    rel = jnp.clip(q_pos[:, None] - k_pos[None, :], -REL_CLIP, REL_CLIP) + REL_CLIP
    s = s + rel_bias[:, rel].astype(jnp.float32)
    qc = q_pos[:, None] // CHUNK
    kc = k_pos[None, :] // CHUNK
    mask = (k_pos[None, :] >= 0) & (kc <= qc) & (kc >= qc - LEFT_CHUNKS)
    p = jax.nn.softmax(jnp.where(mask, s, -jnp.inf), axis=-1)
    return jnp.einsum('bhqk,bkhd->bqhd', p.astype(v.dtype), v)


def conv_module(u_ext, dw_w, dw_b, ln_g, ln_b):
    y = lax.conv_general_dilated(
        u_ext, dw_w[:, None, :], window_strides=(1,), padding='VALID',
        dimension_numbers=('NWC', 'WIO', 'NWC'), feature_group_count=CONV_CH) + dw_b
    return jax.nn.silu(layernorm(y, ln_g, ln_b))


def ab_project(h, w_in):
    B, T, _ = h.shape
    p = h @ w_in
    q, k, v, a, g = jnp.split(
        p, [SB_WIDTH, 2 * SB_WIDTH, 3 * SB_WIDTH, 3 * SB_WIDTH + CONV_CH], axis=-1)
    heads = lambda t: t.reshape(B, T, N_SB_HEADS, HEAD_DIM)
    return heads(q), heads(k), heads(v), a * jax.nn.sigmoid(g)


def ab_mixer_prompt(h, w_in, w_out, dw_w, dw_b, ln_g, ln_b):
    B, T, _ = h.shape
    q, k, v, u = ab_project(h, w_in)
    k_pos = jnp.arange(T)

    def q_block(b):
        start = b * Q_BLOCK
        qb = lax.dynamic_slice_in_dim(q, start, Q_BLOCK, axis=1)
        return stick_breaking(qb, k, v, start + jnp.arange(Q_BLOCK), k_pos)

    a_out = lax.map(q_block, jnp.arange(T // Q_BLOCK))
    a_out = jnp.moveaxis(a_out, 0, 1).reshape(B, T, SB_WIDTH)
    u_ext = jnp.pad(u, ((0, 0), (CONV_STATE, 0), (0, 0)))
    c_out = conv_module(u_ext, dw_w, dw_b, ln_g, ln_b)
    out = jnp.concatenate([a_out, c_out], axis=-1) @ w_out
    return out, k, v, u[:, T - CONV_STATE:]


def ab_mixer_sample(h, cache_k, cache_v, cache_conv, w_in, w_out, dw_w, dw_b, ln_g, ln_b):
    B, T, _ = h.shape
    P = cache_k.shape[1]
    q, k, v, u = ab_project(h, w_in)
    k_all = jnp.concatenate([cache_k, k], axis=1)
    v_all = jnp.concatenate([cache_v, v], axis=1)
    a_out = stick_breaking(q, k_all, v_all, P + jnp.arange(T), jnp.arange(P + T))
    a_out = a_out.reshape(B, T, SB_WIDTH)
    u_ext = jnp.concatenate([cache_conv, u], axis=1)
    c_out = conv_module(u_ext, dw_w, dw_b, ln_g, ln_b)
    out = jnp.concatenate([a_out, c_out], axis=-1) @ w_out
    return out, k, v, u_ext[:, T:]


def c_project(h, w_in):
    B, T, _ = h.shape
    p = (h @ w_in).reshape(B, T, 3, N_C_HEADS, HEAD_DIM)
    return p[:, :, 0], p[:, :, 1], p[:, :, 2]


def c_mixer_prompt(h, w_in, w_out, rel_bias):
    B, T, _ = h.shape
    q, k, v = c_project(h, w_in)
    pad = ((0, 0), (BAND_PAST, 0), (0, 0), (0, 0))
    kp = jnp.pad(k, pad)
    vp = jnp.pad(v, pad)
    band = BAND_PAST + CHUNK

    def one_chunk(c):
        start = c * CHUNK
        qc = lax.dynamic_slice_in_dim(q, start, CHUNK, axis=1)
        kc = lax.dynamic_slice_in_dim(kp, start, band, axis=1)
        vc = lax.dynamic_slice_in_dim(vp, start, band, axis=1)
        return band_attention(qc, kc, vc, start + jnp.arange(CHUNK),
                              start - BAND_PAST + jnp.arange(band), rel_bias)

    o = lax.map(one_chunk, jnp.arange(T // CHUNK))
    o = jnp.moveaxis(o, 0, 1).reshape(B, T, C_WIDTH)
    keep = min(BAND_PAST, T)
    return o @ w_out, k[:, T - keep:], v[:, T - keep:]


def c_mixer_sample(h, cache_k, cache_v, w_in, w_out, rel_bias):
    B, T, _ = h.shape
    W = cache_k.shape[1]
    q, k, v = c_project(h, w_in)
    k_all = jnp.concatenate([cache_k, k], axis=1)
    v_all = jnp.concatenate([cache_v, v], axis=1)
    o = band_attention(q, k_all, v_all, PAST_LEN + jnp.arange(T),
                       PAST_LEN - W + jnp.arange(W + T), rel_bias)
    return o.reshape(B, T, C_WIDTH) @ w_out, k, v


def setup_inputs(seed: int = 0) -> dict:
    key = jax.random.key(seed)
    ks = jax.random.split(key, 24)
    f32 = jnp.float32
    nrm = lambda k, shape, scale: jax.random.normal(k, shape, f32) * scale
    c_win = min(BAND_PAST, PAST_LEN)
    return {
        'x_prompt': nrm(ks[0], (BATCH, SEQ, D_MODEL), 1.0),
        'x_sample': nrm(ks[1], (DEC_BATCH, DEC_SEQ, D_MODEL), 1.0),
        'cache_sb_k': nrm(ks[2], (N_AB, DEC_BATCH, PAST_LEN, N_SB_HEADS, HEAD_DIM), 1.0),
        'cache_sb_v': nrm(ks[3], (N_AB, DEC_BATCH, PAST_LEN, N_SB_HEADS, HEAD_DIM), 1.0),
        'cache_conv': nrm(ks[4], (N_AB, DEC_BATCH, CONV_STATE, CONV_CH), 0.5),
        'cache_band_k': nrm(ks[5], (N_C, DEC_BATCH, c_win, N_C_HEADS, HEAD_DIM), 1.0),
        'cache_band_v': nrm(ks[6], (N_C, DEC_BATCH, c_win, N_C_HEADS, HEAD_DIM), 1.0),
        'norm_mix': 1.0 + nrm(ks[7], (DEPTH, D_MODEL), 0.02),
        'norm_ffn': 1.0 + nrm(ks[8], (DEPTH, D_MODEL), 0.02),
        'norm_final': 1.0 + nrm(ks[9], (D_MODEL,), 0.02),
        'w_in_ab': nrm(ks[10], (N_AB, D_MODEL, AB_IN), D_MODEL ** -0.5),
        'w_out_ab': nrm(ks[11], (N_AB, SB_WIDTH + CONV_CH, D_MODEL), (SB_WIDTH + CONV_CH) ** -0.5),
        'dw_w': nrm(ks[12], (N_AB, CONV_WIDTH, CONV_CH), CONV_WIDTH ** -0.5),
        'dw_b': nrm(ks[13], (N_AB, CONV_CH), 0.02),
        'conv_ln_g': 1.0 + nrm(ks[14], (N_AB, CONV_CH), 0.02),
        'conv_ln_b': nrm(ks[15], (N_AB, CONV_CH), 0.02),
        'w_in_c': nrm(ks[16], (N_C, D_MODEL, C_IN), D_MODEL ** -0.5),
        'w_out_c': nrm(ks[17], (N_C, C_WIDTH, D_MODEL), C_WIDTH ** -0.5),
        'rel_bias': nrm(ks[18], (N_C, N_C_HEADS, N_REL), 0.5),
        'w_up': nrm(ks[19], (DEPTH, D_MODEL, D_FF), D_MODEL ** -0.5),
        'w_down': nrm(ks[20], (DEPTH, D_FF, D_MODEL), D_FF ** -0.5),
    }


def reference(x_prompt, x_sample, cache_sb_k, cache_sb_v, cache_conv, cache_band_k, cache_band_v,
              norm_mix, norm_ffn, norm_final, w_in_ab, w_out_ab, dw_w, dw_b, conv_ln_g, conv_ln_b,
              w_in_c, w_out_c, rel_bias, w_up, w_down):
    xp, xs = x_prompt, x_sample
    sbk_p, sbv_p, conv_p, bk_p, bv_p = [], [], [], [], []
    sbk_s, sbv_s, conv_s, bk_s, bv_s = [], [], [], [], []
    for layer in range(DEPTH):
        i = layer // 2
        hp = rmsnorm(xp, norm_mix[layer])
        hs = rmsnorm(xs, norm_mix[layer])
        if layer % 2 == 0:
            op, k, v, c = ab_mixer_prompt(hp, w_in_ab[i], w_out_ab[i], dw_w[i], dw_b[i],
                                          conv_ln_g[i], conv_ln_b[i])
            sbk_p.append(k); sbv_p.append(v); conv_p.append(c)
            os_, k, v, c = ab_mixer_sample(hs, cache_sb_k[i], cache_sb_v[i], cache_conv[i],
                                           w_in_ab[i], w_out_ab[i], dw_w[i], dw_b[i],
                                           conv_ln_g[i], conv_ln_b[i])
            sbk_s.append(k); sbv_s.append(v); conv_s.append(c)
        else:
            op, k, v = c_mixer_prompt(hp, w_in_c[i], w_out_c[i], rel_bias[i])
            bk_p.append(k); bv_p.append(v)
            os_, k, v = c_mixer_sample(hs, cache_band_k[i], cache_band_v[i],
                                       w_in_c[i], w_out_c[i], rel_bias[i])
            bk_s.append(k); bv_s.append(v)
        xp = xp + op
        xs = xs + os_
        xp = xp + sq_relu_mlp(rmsnorm(xp, norm_ffn[layer]), w_up[layer], w_down[layer])
        xs = xs + sq_relu_mlp(rmsnorm(xs, norm_ffn[layer]), w_up[layer], w_down[layer])
    y_prompt = rmsnorm(xp, norm_final)
    y_sample = rmsnorm(xs, norm_final)
    new_sb_k_prompt = jnp.stack(sbk_p)
    new_sb_v_prompt = jnp.stack(sbv_p)
    new_conv_prompt = jnp.stack(conv_p)
    new_band_k_prompt = jnp.stack(bk_p)
    new_band_v_prompt = jnp.stack(bv_p)
    new_sb_k_sample = jnp.stack(sbk_s)
    new_sb_v_sample = jnp.stack(sbv_s)
    new_conv_sample = jnp.stack(conv_s)
    new_band_k_sample = jnp.stack(bk_s)
    new_band_v_sample = jnp.stack(bv_s)
    return (y_prompt, y_sample, new_sb_k_prompt, new_sb_v_prompt, new_conv_prompt,
            new_band_k_prompt, new_band_v_prompt, new_sb_k_sample, new_sb_v_sample,
            new_conv_sample, new_band_k_sample, new_band_v_sample)
```

```python
import functools

import jax
import jax.numpy as jnp
from jax import lax
from jax.experimental import pallas as pl
from jax.experimental.pallas import tpu as pltpu

F32 = jnp.float32
BF16 = jnp.bfloat16

HEAD_DIM = 64
CHUNK = 64
CHUNK_SHIFT = 6
LEFT_CHUNKS = 8
BAND_PAST = LEFT_CHUNKS * CHUNK
REL_CLIP = 128
CONV_WIDTH = 31
CONV_STATE = CONV_WIDTH - 1
RMS_EPS = 1e-6
LN_EPS = 1e-5
SCALE = HEAD_DIM ** -0.5

V7X_LANES = 128
V7X_MXU_DIM = 256
V7X_VMEM_LIMIT = 56 * 1024 * 1024
HALO_ROWS = 32
CONV_ROWS = 64
BAND_Q = 256
BAND_WIN = BAND_PAST + BAND_Q
REL_FIRST = REL_CLIP + 1 - CHUNK
REL_COLS = 256


def _params(*semantics):
    return pltpu.CompilerParams(dimension_semantics=semantics, vmem_limit_bytes=V7X_VMEM_LIMIT)


def _dot(a, b):
    return jnp.dot(a, b, preferred_element_type=F32)


def _dot_t(a, b):
    return lax.dot_general(a, b, (((1,), (1,)), ((), ())), preferred_element_type=F32)


def _rms_rows(x, g):
    r = lax.rsqrt(jnp.mean(x * x, axis=-1, keepdims=True) + RMS_EPS)
    return (x * r) * g


def _resident(shape):
    return pl.BlockSpec(shape, lambda *_: (0,) * len(shape), pipeline_mode=pl.Buffered(1))


def _split_heads(xp, lane_lo):
    zero = jnp.zeros_like(xp)
    return jnp.concatenate([jnp.where(lane_lo, xp, zero), jnp.where(lane_lo, zero, xp)], axis=0)


def _ab_in_kernel(x_ref, g_ref, w_ref, q_ref, k_ref, v_ref, u_ref, h_ref):
    h_ref[...] = _rms_rows(x_ref[...], g_ref[...]).astype(BF16)
    sw = q_ref.shape[-1]
    cw = u_ref.shape[-1]
    proj = lambda c0, n: _dot(h_ref[...], w_ref[:, c0:c0 + n])
    q_ref[...] = (proj(0, sw) * SCALE).astype(BF16)
    k_ref[...] = proj(sw, sw)
    v_ref[...] = proj(2 * sw, sw)
    u_ref[...] = proj(3 * sw, cw) * jax.nn.sigmoid(proj(3 * sw + cw, cw))


def _ab_in(x, g, w, sw, cw, tm):
    m, d = x.shape
    row = lambda n: pl.BlockSpec((tm, n), lambda i: (i, 0))
    return pl.pallas_call(
        _ab_in_kernel,
        grid=(m // tm,),
        in_specs=[row(d), _resident((1, d)), _resident(w.shape)],
        out_specs=[row(sw), row(sw), row(sw), row(cw)],
        out_shape=[jax.ShapeDtypeStruct((m, sw), BF16), jax.ShapeDtypeStruct((m, sw), F32),
                   jax.ShapeDtypeStruct((m, sw), F32), jax.ShapeDtypeStruct((m, cw), F32)],
        scratch_shapes=[pltpu.VMEM((tm, d), BF16)],
        compiler_params=_params("arbitrary"),
        name="ab_in_proj",
    )(x, g, w)


def _sb_block(q2, kblk, vblk, tri, carry, acc, mask, lane_lo):
    bq = q2.shape[0] // 2
    z = _dot_t(q2, kblk)
    lk = -(jnp.maximum(z, 0.0) + jnp.log1p(jnp.exp(-jnp.abs(z))))
    if mask is not None:
        lk = jnp.where(mask, lk, 0.0)
    hi = lk.astype(BF16)
    lo = (lk - hi.astype(F32)).astype(BF16)
    c = _dot(hi, tri) + _dot(lo, tri)
    w = jnp.exp(z + c + carry)
    if mask is not None:
        w = jnp.where(mask, w, 0.0)
    w = w.astype(BF16)
    wcat = jnp.concatenate([w[:bq], w[bq:]], axis=1)
    acc = acc + _dot(wcat, _split_heads(vblk, lane_lo))
    return carry + c[:, 0:1], acc


def _sb_consts(bq, kb):
    lane_lo = lax.broadcasted_iota(jnp.int32, (1, V7X_LANES), 1) < HEAD_DIM
    tri = jnp.where(lax.broadcasted_iota(jnp.int32, (kb, kb), 0) >= lax.broadcasted_iota(jnp.int32, (kb, kb), 1),
                    1.0, 0.0).astype(BF16)
    assert bq & (bq - 1) == 0
    t_loc = lax.broadcasted_iota(jnp.int32, (2 * bq, kb), 0) & (bq - 1)
    causal = lax.broadcasted_iota(jnp.int32, (2 * bq, kb), 1) < t_loc
    return lane_lo, tri, causal


def _sb_prompt_kernel(q_ref, k_ref, v_ref, o_ref):
    i = pl.program_id(2)
    bq = q_ref.shape[1]
    lane_lo, tri, causal = _sb_consts(bq, bq)
    q2 = _split_heads(q_ref[0], lane_lo)

    def load(j):
        rows = pl.ds(pl.multiple_of(j * bq, bq), bq)
        return k_ref[0, rows, :].astype(BF16), v_ref[0, rows, :].astype(BF16)

    kb, vb = load(i)
    state = _sb_block(q2, kb, vb, tri, jnp.zeros((2 * bq, 1), F32), jnp.zeros((bq, V7X_LANES), F32), causal, lane_lo)

    def body(s, st):
        kb, vb = load(i - 1 - s)
        return _sb_block(q2, kb, vb, tri, st[0], st[1], None, lane_lo)

    _, acc = lax.fori_loop(0, i, body, state)
    o_ref[0] = acc.astype(BF16)


def _sb_prompt(q, k, v):
    b, t, sw = q.shape
    bq = V7X_MXU_DIM
    qspec = pl.BlockSpec((1, bq, V7X_LANES), lambda bi, p, i: (bi, i, p))
    kvspec = pl.BlockSpec((1, t, V7X_LANES), lambda bi, p, i: (bi, 0, p))
    return pl.pallas_call(
        _sb_prompt_kernel,
        grid=(b, sw // V7X_LANES, t // bq),
        in_specs=[qspec, kvspec, kvspec],
        out_specs=qspec,
        out_shape=jax.ShapeDtypeStruct((b, t, sw), BF16),
        compiler_params=_params("arbitrary", "arbitrary", "arbitrary"),
        name="sb_attn_prompt",
    )(q, k, v)


def _sb_sample_kernel(q_ref, kc_ref, vc_ref, kn_ref, vn_ref, o_ref, kpad_ref, vpad_ref):
    bq = q_ref.shape[1]
    kb = kpad_ref.shape[0]
    lane_lo, tri, causal = _sb_consts(bq, kb)
    q2 = _split_heads(q_ref[0], lane_lo)
    kpad_ref[...] = jnp.zeros_like(kpad_ref)
    vpad_ref[...] = jnp.zeros_like(vpad_ref)
    kpad_ref[0:bq, :] = kn_ref[0].astype(BF16)
    vpad_ref[0:bq, :] = vn_ref[0].astype(BF16)
    state = _sb_block(q2, kpad_ref[...], vpad_ref[...], tri, jnp.zeros((2 * bq, 1), F32),
                      jnp.zeros((bq, V7X_LANES), F32), causal, lane_lo)
    nblk = kc_ref.shape[1] // kb

    def body(s, st):
        rows = pl.ds(pl.multiple_of((nblk - 1 - s) * kb, kb), kb)
        return _sb_block(q2, kc_ref[0, rows, :].astype(BF16), vc_ref[0, rows, :].astype(BF16), tri,
                         st[0], st[1], None, lane_lo)

    _, acc = lax.fori_loop(0, nblk, body, state)
    o_ref[0] = acc.astype(BF16)


def _sb_sample(q, kc, vc, kn, vn):
    b, t, sw = q.shape
    past = kc.shape[1]
    new = pl.BlockSpec((1, t, V7X_LANES), lambda bi, p: (bi, 0, p))
    cache = pl.BlockSpec((1, past, V7X_LANES), lambda bi, p: (bi, 0, p))
    return pl.pallas_call(
        _sb_sample_kernel,
        grid=(b, sw // V7X_LANES),
        in_specs=[new, cache, cache, new, new],
        out_specs=new,
        out_shape=jax.ShapeDtypeStruct((b, t, sw), BF16),
        scratch_shapes=[pltpu.VMEM((V7X_MXU_DIM, V7X_LANES), BF16)] * 2,
        compiler_params=_params("arbitrary", "arbitrary"),
        name="sb_attn_sample",
    )(q, kc, vc, kn, vn)


def _conv_kernel(u_ref, halo_ref, dw_ref, b_ref, g_ref, be_ref, c_ref, ext_ref, *, halo_is_history):
    tm = u_ref.shape[1]
    halo = halo_ref[0]
    if not halo_is_history:
        halo = jnp.where(pl.program_id(1) > 0, halo, 0.0)
    ext_ref[0:HALO_ROWS, :] = halo
    ext_ref[HALO_ROWS:, :] = u_ref[0]
    first = HALO_ROWS - CONV_STATE
    rows = min(CONV_ROWS, tm)
    for r0 in range(0, tm, rows):
        y = ext_ref[r0 + first:r0 + first + rows, :] * dw_ref[0:1, :] + b_ref[...]
        for w in range(1, CONV_WIDTH):
            y = y + ext_ref[r0 + first + w:r0 + first + w + rows, :] * dw_ref[w:w + 1, :]
        mu = jnp.mean(y, axis=-1, keepdims=True)
        var = jnp.mean(jnp.square(y - mu), axis=-1, keepdims=True)
        yn = ((y - mu) * lax.rsqrt(var + LN_EPS)) * g_ref[...] + be_ref[...]
        c_ref[0, r0:r0 + rows, :] = (yn * jax.nn.sigmoid(yn)).astype(BF16)


def _conv(u, halo_src, dw, b, g, be, tm, halo_is_history):
    bsz, t, c = u.shape
    if halo_is_history:
        halo_spec = pl.BlockSpec((1, HALO_ROWS, c), lambda bi, ti: (bi, 0, 0))
    else:
        per = tm // HALO_ROWS
        halo_spec = pl.BlockSpec((1, HALO_ROWS, c), lambda bi, ti: (bi, jnp.maximum(ti * per - 1, 0), 0))
    tile = pl.BlockSpec((1, tm, c), lambda bi, ti: (bi, ti, 0))
    vec = _resident((1, c))
    return pl.pallas_call(
        functools.partial(_conv_kernel, halo_is_history=halo_is_history),
        grid=(bsz, t // tm),
        in_specs=[tile, halo_spec, _resident(dw.shape), vec, vec, vec],
        out_specs=tile,
        out_shape=jax.ShapeDtypeStruct((bsz, t, c), BF16),
        scratch_shapes=[pltpu.VMEM((HALO_ROWS + tm, c), F32)],
        compiler_params=_params("arbitrary", "arbitrary"),
        name="conv_module",
    )(u, halo_src, dw, b, g, be)


def _out_mlp_kernel(x_ref, m0_ref, m1_ref, wo_ref, g_ref, wu_ref, wd_ref, gf_ref, y_ref, h_ref,
                    *, final_norm, ff_chunk):
    half = m0_ref.shape[-1]
    x1 = x_ref[...] + _dot(m0_ref[...], wo_ref[0:half, :]) + _dot(m1_ref[...], wo_ref[half:, :])
    h_ref[...] = _rms_rows(x1, g_ref[...]).astype(BF16)
    y_ref[...] = x1
    for c0 in range(0, wu_ref.shape[1], ff_chunk):
        up = _dot(h_ref[...], wu_ref[:, c0:c0 + ff_chunk])
        act = jnp.square(jnp.maximum(up, 0.0)).astype(BF16)
        y_ref[...] += _dot(act, wd_ref[c0:c0 + ff_chunk, :])
    if final_norm:
        y_ref[...] = _rms_rows(y_ref[...], gf_ref[...])


def _out_mlp(x, m0, m1, m1_block, wo, g, wu, wd, gf, tm, final_norm):
    m, d = x.shape
    half = wo.shape[0] // 2
    row = pl.BlockSpec((tm, d), lambda i: (i, 0))
    return pl.pallas_call(
        functools.partial(_out_mlp_kernel, final_norm=final_norm, ff_chunk=512),
        grid=(m // tm,),
        in_specs=[row, pl.BlockSpec((tm, half), lambda i: (i, 0)), pl.BlockSpec((tm, half), lambda i: (i, m1_block)),
                  _resident(wo.shape), _resident((1, d)), _resident(wu.shape), _resident(wd.shape), _resident((1, d))],
        out_specs=row,
        out_shape=jax.ShapeDtypeStruct((m, d), F32),
        scratch_shapes=[pltpu.VMEM((tm, d), BF16)],
        compiler_params=_params("arbitrary"),
        name="out_proj_mlp",
    )(x, m0, m1, wo, g, wu, wd, gf)


def _c_in_kernel(x_ref, g_ref, w_ref, q_ref, k_ref, v_ref, kt_ref, vt_ref, h_ref):
    h_ref[...] = _rms_rows(x_ref[...], g_ref[...]).astype(BF16)
    cw = q_ref.shape[-1]
    half = cw // 2
    for c0 in range(0, cw, half):
        cols = slice(c0, c0 + half)
        q_ref[:, cols] = (_dot(h_ref[...], w_ref[:, c0:c0 + half]) * SCALE).astype(BF16)
        kt_ref[:, cols] = _dot(h_ref[...], w_ref[:, cw + c0:cw + c0 + half])
        k_ref[:, cols] = kt_ref[:, cols].astype(BF16)
        vt_ref[:, cols] = _dot(h_ref[...], w_ref[:, 2 * cw + c0:2 * cw + c0 + half])
        v_ref[:, cols] = vt_ref[:, cols].astype(BF16)


def _c_in(x, g, w, t, keep, tm):
    m, d = x.shape
    cw = w.shape[1] // 3
    row = lambda n: pl.BlockSpec((tm, n), lambda i: (i, 0))
    if keep == t:
        tail_map = lambda i: (i, 0)
    else:
        per_t, per_keep = t // tm, keep // tm
        tail_map = lambda i: ((i // per_t) * per_keep + jnp.maximum(i % per_t - (per_t - per_keep), 0), 0)
    tail = pl.BlockSpec((tm, cw), tail_map)
    nb = m // t
    return pl.pallas_call(
        _c_in_kernel,
        grid=(m // tm,),
        in_specs=[row(d), _resident((1, d)), _resident(w.shape)],
        out_specs=[row(cw), row(cw), row(cw), tail, tail],
        out_shape=[jax.ShapeDtypeStruct((m, cw), BF16)] * 3 + [jax.ShapeDtypeStruct((nb * keep, cw), F32)] * 2,
        scratch_shapes=[pltpu.VMEM((tm, d), BF16)],
        compiler_params=_params("arbitrary"),
        name="c_in_proj",
    )(x, g, w)


def _bias_kernel(rb_ref, o_ref):
    rb = rb_ref[...]
    hi = rb.astype(BF16)
    r1 = rb - hi.astype(F32)
    mid = r1.astype(BF16)
    lo = (r1 - mid.astype(F32)).astype(BF16)
    ncol, win = rb.shape[1], o_ref.shape[2]
    s = lax.broadcasted_iota(jnp.int32, (1, win), 1)
    kk = lax.broadcasted_iota(jnp.int32, (ncol, win), 0)

    def body(r, carry):
        idx = jnp.clip(BAND_PAST + r - s, -REL_CLIP, REL_CLIP) + REL_CLIP - REL_FIRST
        kc, qc = s >> CHUNK_SHIFT, r >> CHUNK_SHIFT
        visible = (kc >= qc) & (kc <= qc + LEFT_CHUNKS)
        onehot = jnp.where((kk == idx) & visible, 1.0, 0.0).astype(BF16)
        row = _dot(hi, onehot) + _dot(mid, onehot) + _dot(lo, onehot)
        o_ref[r] = jnp.where(visible, row, -jnp.inf)
        return carry

    lax.fori_loop(0, o_ref.shape[0], body, 0)


def _bias_tile(rel_bias):
    h = rel_bias.shape[0]
    cols = rel_bias[:, REL_FIRST:]
    cols = jnp.pad(cols, ((0, 0), (0, REL_COLS - cols.shape[1])))
    tile = pl.pallas_call(
        _bias_kernel,
        out_shape=jax.ShapeDtypeStruct((BAND_Q, h, BAND_WIN), F32),
        compiler_params=pltpu.CompilerParams(vmem_limit_bytes=V7X_VMEM_LIMIT),
        name="band_bias_tile",
    )(cols)
    return jnp.transpose(tile, (1, 0, 2))


def _band_core(q_pair, kw, vw, bias_ref, valid):
    bq = q_pair.shape[0]
    lane_lo = lax.broadcasted_iota(jnp.int32, (1, V7X_LANES), 1) < HEAD_DIM
    z = _dot_t(_split_heads(q_pair, lane_lo), kw)
    s = z + jnp.concatenate([bias_ref[0], bias_ref[1]], axis=0)
    s = jnp.where(valid, s, -jnp.inf)
    e = jnp.exp(s - jnp.max(s, axis=-1, keepdims=True))
    inv = 1.0 / jnp.sum(e, axis=-1, keepdims=True)
    e = e.astype(BF16)
    o = _dot(jnp.concatenate([e[:bq], e[bq:]], axis=1), _split_heads(vw, lane_lo))
    return (o * jnp.where(lane_lo, inv[:bq], inv[bq:])).astype(BF16)


def _band_prompt_kernel(q_ref, k_ref, v_ref, bias_ref, o_ref, kpad_ref, vpad_ref):
    i = pl.program_id(2)
    bq = q_ref.shape[1]
    win = bias_ref.shape[2]
    past = win - bq

    @pl.when(i == 0)
    def _():
        kpad_ref[0:past, :] = jnp.zeros((past, V7X_LANES), BF16)
        vpad_ref[0:past, :] = jnp.zeros((past, V7X_LANES), BF16)
        kpad_ref[past:, :] = k_ref[0]
        vpad_ref[past:, :] = v_ref[0]

    rows = pl.ds(pl.multiple_of(i * bq, bq), win)
    valid = lax.broadcasted_iota(jnp.int32, (1, win), 1) >= past - i * bq
    o_ref[0] = _band_core(q_ref[0], kpad_ref[rows, :], vpad_ref[rows, :], bias_ref, valid)


def _band_prompt(q, k, v, bias):
    b, t, cw = q.shape
    bq, win = bias.shape[1], bias.shape[2]
    qspec = pl.BlockSpec((1, bq, V7X_LANES), lambda p, bi, i: (bi, i, p))
    kvspec = pl.BlockSpec((1, t, V7X_LANES), lambda p, bi, i: (bi, 0, p))
    return pl.pallas_call(
        _band_prompt_kernel,
        grid=(cw // V7X_LANES, b, t // bq),
        in_specs=[qspec, kvspec, kvspec, pl.BlockSpec((2, bq, win), lambda p, bi, i: (p, 0, 0))],
        out_specs=qspec,
        out_shape=jax.ShapeDtypeStruct((b, t, cw), BF16),
        scratch_shapes=[pltpu.VMEM((win - bq + t, V7X_LANES), BF16)] * 2,
        compiler_params=_params("arbitrary", "arbitrary", "arbitrary"),
        name="band_attn_prompt",
    )(q, k, v, bias)


def _band_sample_kernel(q_ref, kc_ref, vc_ref, kn_ref, vn_ref, bias_ref, o_ref, kpad_ref, vpad_ref):
    t = q_ref.shape[1]
    past = kc_ref.shape[1]
    win = bias_ref.shape[2]
    kpad_ref[0:past, :] = kc_ref[0].astype(BF16)
    vpad_ref[0:past, :] = vc_ref[0].astype(BF16)
    kpad_ref[past:past + t, :] = kn_ref[0]
    vpad_ref[past:past + t, :] = vn_ref[0]
    kpad_ref[past + t:, :] = jnp.zeros((win - past - t, V7X_LANES), BF16)
    vpad_ref[past + t:, :] = jnp.zeros((win - past - t, V7X_LANES), BF16)
    valid = lax.broadcasted_iota(jnp.int32, (1, win), 1) < past + t
    o_ref[0] = _band_core(q_ref[0], kpad_ref[...], vpad_ref[...], bias_ref, valid)


def _band_sample(q, kc, vc, kn, vn, bias):
    b, t, cw = q.shape
    past, win = kc.shape[1], bias.shape[2]
    new = pl.BlockSpec((1, t, V7X_LANES), lambda p, bi: (bi, 0, p))
    cache = pl.BlockSpec((1, past, V7X_LANES), lambda p, bi: (bi, 0, p))
    return pl.pallas_call(
        _band_sample_kernel,
        grid=(cw // V7X_LANES, b),
        in_specs=[new, cache, cache, new, new, pl.BlockSpec((2, t, win), lambda p, bi: (p, 0, 0))],
        out_specs=new,
        out_shape=jax.ShapeDtypeStruct((b, t, cw), BF16),
        scratch_shapes=[pltpu.VMEM((win, V7X_LANES), BF16)] * 2,
        compiler_params=_params("arbitrary", "arbitrary"),
        name="band_attn_sample",
    )(q, kc, vc, kn, vn, bias)


def kernel(x_prompt, x_sample, cache_sb_k, cache_sb_v, cache_conv, cache_band_k, cache_band_v, norm_mix, norm_ffn,
           norm_final, w_in_ab, w_out_ab, dw_w, dw_b, conv_ln_g, conv_ln_b, w_in_c, w_out_c, rel_bias, w_up, w_down):
    b, t, d = x_prompt.shape
    bs, ts, _ = x_sample.shape
    n_sb, n_c = cache_sb_k.shape[3], cache_band_k.shape[3]
    sw, cc, cw = n_sb * HEAD_DIM, dw_w.shape[2], n_c * HEAD_DIM
    past = cache_sb_k.shape[2]
    band_past = cache_band_k.shape[2]
    keep = min(BAND_PAST, t)
    assert w_in_ab.shape[0] == 1 and w_in_c.shape[0] == 1 and norm_mix.shape[0] == 2
    assert sw == cc and sw + cc == d and cw == d and w_in_ab.shape[2] == 3 * sw + 2 * cc
    assert t % 512 == 0 and ts % 16 == 0 and ts <= V7X_MXU_DIM
    assert past % V7X_MXU_DIM == 0 and band_past == BAND_PAST and band_past + ts <= BAND_WIN

    tm_p, tm_s = 512, bs * ts
    vec = lambda a: a.reshape(1, -1)
    w_in0, w_out0 = w_in_ab[0].astype(BF16), w_out_ab[0].astype(BF16)
    w_in1, w_out1 = w_in_c[0].astype(BF16), w_out_c[0].astype(BF16)
    w_up_b, w_down_b = w_up.astype(BF16), w_down.astype(BF16)
    dw_pad = jnp.pad(dw_w[0], ((0, HALO_ROWS - CONV_WIDTH), (0, 0)))
    conv_vecs = (vec(dw_b[0]), vec(conv_ln_g[0]), vec(conv_ln_b[0]))
    bias = _bias_tile(rel_bias[0])

    def layer0(x, tm, attn, conv):
        q, k, v, u = _ab_in(x, vec(norm_mix[0]), w_in0, sw, cc, tm)
        a = attn(q, k, v)
        c = conv(u)
        x2 = _out_mlp(x, a, c, 0, w_out0, vec(norm_ffn[0]), w_up_b[0], w_down_b[0], vec(norm_final), tm, False)
        return x2, k, v, u

    def layer1(x, tm, tlen, tail, attn):
        q, k, v, kt, vt = _c_in(x, vec(norm_mix[1]), w_in1, tlen, tail, tm)
        o = attn(q, k, v)
        y = _out_mlp(x, o, o, 1, w_out1, vec(norm_ffn[1]), w_up_b[1], w_down_b[1], vec(norm_final), tm, True)
        return y, kt, vt

    shp = lambda z, n: z.reshape(b, t, n)
    xp2, kp, vp, up = layer0(
        x_prompt.reshape(b * t, d), tm_p,
        lambda q, k, v: _sb_prompt(shp(q, sw), shp(k, sw), shp(v, sw)).reshape(b * t, sw),
        lambda u: _conv(shp(u, cc), shp(u, cc), dw_pad, *conv_vecs, 256, False).reshape(b * t, cc))
    yp, bkp, bvp = layer1(
        xp2, tm_p, t, keep,
        lambda q, k, v: _band_prompt(shp(q, cw), shp(k, cw), shp(v, cw), bias).reshape(b * t, cw))

    shs = lambda z, n: z.reshape(bs, ts, n)
    conv_hist = jnp.pad(cache_conv[0], ((0, 0), (HALO_ROWS - CONV_STATE, 0), (0, 0)))
    ck, cv = cache_sb_k[0].reshape(bs, past, sw), cache_sb_v[0].reshape(bs, past, sw)
    xs2, ks, vs, us = layer0(
        x_sample.reshape(bs * ts, d), tm_s,
        lambda q, k, v: _sb_sample(shs(q, sw), ck, cv, shs(k, sw), shs(v, sw)).reshape(bs * ts, sw),
        lambda u: _conv(shs(u, cc), conv_hist, dw_pad, *conv_vecs, ts, True).reshape(bs * ts, cc))
    bck, bcv = cache_band_k[0].reshape(bs, band_past, cw), cache_band_v[0].reshape(bs, band_past, cw)
    ys, bks, bvs = layer1(
        xs2, tm_s, ts, ts,
        lambda q, k, v: _band_sample(shs(q, cw), bck, bcv, shs(k, cw), shs(v, cw), bias).reshape(bs * ts, cw))

    new_conv_p = up.reshape(b, t, cc)[:, t - CONV_STATE:]
    new_conv_s = jnp.concatenate([cache_conv[0], us.reshape(bs, ts, cc)], axis=1)[:, ts:]
    return (yp.reshape(b, t, d), ys.reshape(bs, ts, d),
            kp.reshape(1, b, t, n_sb, HEAD_DIM), vp.reshape(1, b, t, n_sb, HEAD_DIM), new_conv_p[None],
            bkp.reshape(1, b, keep, n_c, HEAD_DIM), bvp.reshape(1, b, keep, n_c, HEAD_DIM),
            ks.reshape(1, bs, ts, n_sb, HEAD_DIM), vs.reshape(1, bs, ts, n_sb, HEAD_DIM), new_conv_s[None],
            bks.reshape(1, bs, ts, n_c, HEAD_DIM), bvs.reshape(1, bs, ts, n_c, HEAD_DIM))
```

```python
import functools

import jax
import jax.numpy as jnp
from jax import lax
from jax.experimental import pallas as pl
from jax.experimental.pallas import tpu as pltpu

F32 = jnp.float32
BF16 = jnp.bfloat16

HEAD_DIM = 64
CHUNK = 64
CHUNK_SHIFT = 6
LEFT_CHUNKS = 8
BAND_PAST = LEFT_CHUNKS * CHUNK
REL_CLIP = 128
CONV_WIDTH = 31
CONV_STATE = CONV_WIDTH - 1
RMS_EPS = 1e-6
LN_EPS = 1e-5
SCALE = HEAD_DIM ** -0.5

V7X_LANES = 128
SUBLANES = 8
V7X_MXU_DIM = 256
V7X_VMEM_LIMIT = 56 * 1024 * 1024
ROW_TILE = 512
FF_CHUNK = 512
CONV_TILE = 256
HALO_ROWS = 32
CONV_ROWS = 64
PAIRS_PER_STEP = 4
BAND_Q = 256
BAND_WIN = BAND_PAST + BAND_Q
REL_FIRST = REL_CLIP + 1 - CHUNK
REL_COLS = 256


def _params(*semantics):
    return pltpu.CompilerParams(dimension_semantics=semantics, vmem_limit_bytes=V7X_VMEM_LIMIT)


def _dot(a, b):
    return jnp.dot(a, b, preferred_element_type=F32)


def _dot_t(a, b):
    return lax.dot_general(a, b, (((1,), (1,)), ((), ())), preferred_element_type=F32)


def _rms_rows(x, g):
    r = lax.rsqrt(jnp.mean(x * x, axis=-1, keepdims=True) + RMS_EPS)
    return (x * r) * g


def _resident(shape):
    return pl.BlockSpec(shape, lambda *_: (0,) * len(shape), pipeline_mode=pl.Buffered(1))


def _lane_lo():
    return lax.broadcasted_iota(jnp.int32, (1, V7X_LANES), 1) < HEAD_DIM


def _split_heads(xp, lane_lo):
    zero = jnp.zeros_like(xp)
    return jnp.concatenate([jnp.where(lane_lo, xp, zero), jnp.where(lane_lo, zero, xp)], axis=0)


def _pair(p):
    return slice(p * V7X_LANES, (p + 1) * V7X_LANES)


def _ab_in_kernel(x_ref, g_ref, w_ref, q_ref, kb_ref, vb_ref, k_ref, v_ref, u_ref, h_ref):
    h_ref[...] = _rms_rows(x_ref[...], g_ref[...]).astype(BF16)
    sw = q_ref.shape[-1]
    cw = u_ref.shape[-1]
    proj = lambda c0, n: _dot(h_ref[...], w_ref[:, c0:c0 + n])
    q_ref[...] = (proj(0, sw) * SCALE).astype(BF16)
    for c0, b_ref, f_ref in ((sw, kb_ref, k_ref), (2 * sw, vb_ref, v_ref)):
        f_ref[...] = proj(c0, sw)
        b_ref[...] = f_ref[...].astype(BF16)
    u_ref[...] = proj(3 * sw, cw) * jax.nn.sigmoid(proj(3 * sw + cw, cw))


def _ab_in(x, g, w, sw, cw, tm):
    m, d = x.shape
    row = lambda n: pl.BlockSpec((tm, n), lambda i: (i, 0))
    return pl.pallas_call(
        _ab_in_kernel,
        grid=(m // tm,),
        in_specs=[row(d), _resident((1, d)), _resident(w.shape)],
        out_specs=[row(sw)] * 5 + [row(cw)],
        out_shape=[jax.ShapeDtypeStruct((m, sw), BF16)] * 3 + [jax.ShapeDtypeStruct((m, sw), F32)] * 2
        + [jax.ShapeDtypeStruct((m, cw), F32)],
        scratch_shapes=[pltpu.VMEM((tm, d), BF16)],
        compiler_params=_params("arbitrary"),
        name="ab_in_proj",
    )(x, g, w)


def _softplus(z):
    return jnp.maximum(z, 0.0) + jnp.log(1.0 + jnp.exp(-jnp.abs(z)))


def _sb_block(q2, kblk, vblk, tri2, carry, acc, mask, lane_lo):
    bq = q2.shape[0] // 2
    z = _dot_t(q2, kblk)
    sp = _softplus(z)
    if mask is not None:
        sp = jnp.where(mask, sp, 0.0)
    hi = sp.astype(BF16)
    lo = (sp - hi.astype(F32)).astype(BF16)
    c = _dot(jnp.concatenate([hi, lo], axis=1), tri2)
    w = jnp.exp(z - c - carry)
    if mask is not None:
        w = jnp.where(mask, w, 0.0)
    w = w.astype(BF16)
    wcat = jnp.concatenate([w[:bq], w[bq:]], axis=1)
    acc = acc + _dot(wcat, _split_heads(vblk, lane_lo))
    return carry + c[:, 0:1], acc


def _sb_consts(bq, kb):
    tri = jnp.where((lax.broadcasted_iota(jnp.int32, (2 * kb, kb), 0) & (kb - 1))
                    >= lax.broadcasted_iota(jnp.int32, (2 * kb, kb), 1), 1.0, 0.0).astype(BF16)
    assert bq & (bq - 1) == 0
    t_loc = lax.broadcasted_iota(jnp.int32, (2 * bq, kb), 0) & (bq - 1)
    causal = lax.broadcasted_iota(jnp.int32, (2 * bq, kb), 1) < t_loc
    return _lane_lo(), tri, causal


def _sb_zero_state(bq):
    return jnp.zeros((2 * bq, 1), F32), jnp.zeros((bq, V7X_LANES), F32)


def _sb_prompt_kernel(q_ref, k_ref, v_ref, o_ref):
    i = pl.program_id(2)
    bq = q_ref.shape[1]
    npair = q_ref.shape[2] // V7X_LANES
    lane_lo, tri, causal = _sb_consts(bq, bq)
    q2 = [_split_heads(q_ref[0, :, _pair(p)], lane_lo) for p in range(npair)]

    def sweep(j, states, mask):
        rows = pl.ds(pl.multiple_of(j * bq, bq), bq)
        return tuple(_sb_block(q2[p], k_ref[0, rows, _pair(p)], v_ref[0, rows, _pair(p)], tri, *states[p], mask, lane_lo)
                     for p in range(npair))

    states = sweep(i, (_sb_zero_state(bq),) * npair, causal)
    states = lax.fori_loop(0, i, lambda s, st: sweep(i - 1 - s, st, None), states)
    for p in range(npair):
        o_ref[0, :, _pair(p)] = states[p][1].astype(BF16)


def _sb_prompt(q, k, v):
    b, t, sw = q.shape
    bq = V7X_MXU_DIM
    width = PAIRS_PER_STEP * V7X_LANES
    qspec = pl.BlockSpec((1, bq, width), lambda bi, p, i: (bi, i, p))
    kvspec = pl.BlockSpec((1, t, width), lambda bi, p, i: (bi, 0, p))
    return pl.pallas_call(
        _sb_prompt_kernel,
        grid=(b, sw // width, t // bq),
        in_specs=[qspec, kvspec, kvspec],
        out_specs=qspec,
        out_shape=jax.ShapeDtypeStruct((b, t, sw), BF16),
        compiler_params=_params("arbitrary", "arbitrary", "arbitrary"),
        name="sb_attn_prompt",
    )(q, k, v)


def _sb_sample_kernel(q_ref, kc_ref, vc_ref, kn_ref, vn_ref, o_ref, kpad_ref, vpad_ref):
    bq = q_ref.shape[1]
    kb = kpad_ref.shape[0]
    npair = q_ref.shape[2] // V7X_LANES
    nblk = kc_ref.shape[1] // kb
    lane_lo, tri2, causal = _sb_consts(bq, kb)
    kpad_ref[...] = jnp.zeros_like(kpad_ref)
    vpad_ref[...] = jnp.zeros_like(vpad_ref)
    kpad_ref[0:bq, :] = kn_ref[0]
    vpad_ref[0:bq, :] = vn_ref[0]
    for p in range(npair):
        q2 = _split_heads(q_ref[0, :, _pair(p)], lane_lo)
        cache_rows = [slice(j * kb, (j + 1) * kb) for j in reversed(range(nblk))]
        kblocks = [kpad_ref[:, _pair(p)]] + [kc_ref[0, r, _pair(p)].astype(BF16) for r in cache_rows]
        vblocks = [vpad_ref[:, _pair(p)]] + [vc_ref[0, r, _pair(p)].astype(BF16) for r in cache_rows]
        z = [_dot_t(q2, kblk) for kblk in kblocks]
        sp = [_softplus(zb) for zb in z]
        sp[0] = jnp.where(causal, sp[0], 0.0)
        sp_all = jnp.concatenate(sp, axis=0)
        hi = sp_all.astype(BF16)
        lo = (sp_all - hi.astype(F32)).astype(BF16)
        c_all = _dot(jnp.concatenate([hi, lo], axis=1), tri2)
        carry = jnp.zeros((2 * bq, 1), F32)
        ws = []
        for j, zb in enumerate(z):
            c = c_all[j * 2 * bq:(j + 1) * 2 * bq]
            w = jnp.exp(zb - c - carry)
            if j == 0:
                w = jnp.where(causal, w, 0.0)
            carry = carry + c[:, 0:1]
            w = w.astype(BF16)
            ws.append(jnp.concatenate([w[:bq], w[bq:]], axis=1))
        vcat = jnp.concatenate([_split_heads(vblk, lane_lo) for vblk in vblocks], axis=0)
        o_ref[0, :, _pair(p)] = _dot(jnp.concatenate(ws, axis=1), vcat).astype(BF16)


def _sb_sample(q, kc, vc, kn, vn):
    b, t, sw = q.shape
    past = kc.shape[1]
    new = pl.BlockSpec((1, t, sw), lambda bi: (bi, 0, 0))
    cache = pl.BlockSpec((1, past, sw), lambda bi: (bi, 0, 0))
    return pl.pallas_call(
        _sb_sample_kernel,
        grid=(b,),
        in_specs=[new, cache, cache, new, new],
        out_specs=new,
        out_shape=jax.ShapeDtypeStruct((b, t, sw), BF16),
        scratch_shapes=[pltpu.VMEM((V7X_MXU_DIM, sw), BF16)] * 2,
        compiler_params=_params("arbitrary"),
        name="sb_attn_sample",
    )(q, kc, vc, kn, vn)


def _conv_kernel(u_ref, halo_ref, dw_ref, b_ref, g_ref, be_ref, c_ref, ext_ref, sh_ref, *, halo_is_history):
    tm = u_ref.shape[1]
    halo = halo_ref[0]
    if not halo_is_history:
        halo = jnp.where(pl.program_id(1) > 0, halo, 0.0)
    ext_ref[0:HALO_ROWS, :] = halo
    ext_ref[HALO_ROWS:, :] = u_ref[0]
    n = sh_ref.shape[1]
    for s in range(1, SUBLANES):
        sh_ref[s - 1] = ext_ref[s:s + n, :]
    first = HALO_ROWS - CONV_STATE
    rows = min(CONV_ROWS, tm)
    for r0 in range(0, tm, rows):
        y = b_ref[...]
        for w in range(CONV_WIDTH):
            s = (first + w) % SUBLANES
            a = r0 + first + w - s
            tap = ext_ref[a:a + rows, :] if s == 0 else sh_ref[s - 1, a:a + rows, :]
            y = y + tap * dw_ref[w:w + 1, :]
        mu = jnp.mean(y, axis=-1, keepdims=True)
        var = jnp.mean(jnp.square(y - mu), axis=-1, keepdims=True)
        yn = ((y - mu) * lax.rsqrt(var + LN_EPS)) * g_ref[...] + be_ref[...]
        c_ref[0, r0:r0 + rows, :] = (yn * jax.nn.sigmoid(yn)).astype(BF16)


def _conv(u, halo_src, dw, b, g, be, tm, halo_is_history):
    bsz, t, c = u.shape
    if halo_is_history:
        halo_spec = pl.BlockSpec((1, HALO_ROWS, c), lambda bi, ti: (bi, 0, 0))
    else:
        per = tm // HALO_ROWS
        halo_spec = pl.BlockSpec((1, HALO_ROWS, c), lambda bi, ti: (bi, jnp.maximum(ti * per - 1, 0), 0))
    tile = pl.BlockSpec((1, tm, c), lambda bi, ti: (bi, ti, 0))
    vec = _resident((1, c))
    return pl.pallas_call(
        functools.partial(_conv_kernel, halo_is_history=halo_is_history),
        grid=(bsz, t // tm),
        in_specs=[tile, halo_spec, _resident(dw.shape), vec, vec, vec],
        out_specs=tile,
        out_shape=jax.ShapeDtypeStruct((bsz, t, c), BF16),
        scratch_shapes=[pltpu.VMEM((HALO_ROWS + tm, c), F32),
                        pltpu.VMEM((SUBLANES - 1, HALO_ROWS + tm - SUBLANES, c), F32)],
        compiler_params=_params("arbitrary", "arbitrary"),
        name="conv_module",
    )(u, halo_src, dw, b, g, be)


def _out_mlp_kernel(x_ref, m0_ref, m1_ref, wo_ref, g_ref, wu_ref, wd_ref, gf_ref, y_ref, h_ref, *, final_norm):
    half = m0_ref.shape[-1]
    x1 = x_ref[...] + _dot(m0_ref[...], wo_ref[0:half, :]) + _dot(m1_ref[...], wo_ref[half:, :])
    h_ref[...] = _rms_rows(x1, g_ref[...]).astype(BF16)
    y_ref[...] = x1
    for c0 in range(0, wu_ref.shape[1], FF_CHUNK):
        up = _dot(h_ref[...], wu_ref[:, c0:c0 + FF_CHUNK])
        act = jnp.square(jnp.maximum(up, 0.0)).astype(BF16)
        y_ref[...] += _dot(act, wd_ref[c0:c0 + FF_CHUNK, :])
    if final_norm:
        y_ref[...] = _rms_rows(y_ref[...], gf_ref[...])


def _out_mlp(x, m0, m1, m1_block, wo, g, wu, wd, gf, tm, final_norm):
    m, d = x.shape
    half = wo.shape[0] // 2
    row = pl.BlockSpec((tm, d), lambda i: (i, 0))
    return pl.pallas_call(
        functools.partial(_out_mlp_kernel, final_norm=final_norm),
        grid=(m // tm,),
        in_specs=[row, pl.BlockSpec((tm, half), lambda i: (i, 0)), pl.BlockSpec((tm, half), lambda i: (i, m1_block)),
                  _resident(wo.shape), _resident((1, d)), _resident(wu.shape), _resident(wd.shape), _resident((1, d))],
        out_specs=row,
        out_shape=jax.ShapeDtypeStruct((m, d), F32),
        scratch_shapes=[pltpu.VMEM((tm, d), BF16)],
        compiler_params=_params("arbitrary"),
        name="out_proj_mlp",
    )(x, m0, m1, wo, g, wu, wd, gf)


def _c_in_kernel(x_ref, g_ref, w_ref, q_ref, k_ref, v_ref, kt_ref, vt_ref, h_ref):
    h_ref[...] = _rms_rows(x_ref[...], g_ref[...]).astype(BF16)
    cw = q_ref.shape[-1]
    half = cw // 2
    for c0 in range(0, cw, half):
        cols = slice(c0, c0 + half)
        q_ref[:, cols] = (_dot(h_ref[...], w_ref[:, c0:c0 + half]) * SCALE).astype(BF16)
        for base, b_ref, f_ref in ((cw, k_ref, kt_ref), (2 * cw, v_ref, vt_ref)):
            f_ref[:, cols] = _dot(h_ref[...], w_ref[:, base + c0:base + c0 + half])
            b_ref[:, cols] = f_ref[:, cols].astype(BF16)


def _c_in(x, g, w, t, keep, tm):
    m, d = x.shape
    cw = w.shape[1] // 3
    row = lambda n: pl.BlockSpec((tm, n), lambda i: (i, 0))
    if keep == t:
        tail_map = lambda i: (i, 0)
    else:
        per_t, per_keep = t // tm, keep // tm
        tail_map = lambda i: ((i // per_t) * per_keep + jnp.maximum(i % per_t - (per_t - per_keep), 0), 0)
    tail = pl.BlockSpec((tm, cw), tail_map)
    nb = m // t
    return pl.pallas_call(
        _c_in_kernel,
        grid=(m // tm,),
        in_specs=[row(d), _resident((1, d)), _resident(w.shape)],
        out_specs=[row(cw), row(cw), row(cw), tail, tail],
        out_shape=[jax.ShapeDtypeStruct((m, cw), BF16)] * 3 + [jax.ShapeDtypeStruct((nb * keep, cw), F32)] * 2,
        scratch_shapes=[pltpu.VMEM((tm, d), BF16)],
        compiler_params=_params("arbitrary"),
        name="c_in_proj",
    )(x, g, w)


def _bias_kernel(rb_ref, o_ref):
    rb = rb_ref[...]
    hi = rb.astype(BF16)
    r1 = rb - hi.astype(F32)
    mid = r1.astype(BF16)
    lo = (r1 - mid.astype(F32)).astype(BF16)
    ncol, win = rb.shape[1], o_ref.shape[2]
    s = lax.broadcasted_iota(jnp.int32, (1, win), 1)
    kk = lax.broadcasted_iota(jnp.int32, (ncol, win), 0)

    def body(r, carry):
        idx = jnp.clip(BAND_PAST + r - s, -REL_CLIP, REL_CLIP) + REL_CLIP - REL_FIRST
        kc, qc = s >> CHUNK_SHIFT, r >> CHUNK_SHIFT
        visible = (kc >= qc) & (kc <= qc + LEFT_CHUNKS)
        onehot = jnp.where((kk == idx) & visible, 1.0, 0.0).astype(BF16)
        row = _dot(hi, onehot) + _dot(mid, onehot) + _dot(lo, onehot)
        o_ref[r] = jnp.where(visible, row, -jnp.inf)
        return carry

    lax.fori_loop(0, o_ref.shape[0], body, 0)


def _bias_tile(rel_bias):
    h = rel_bias.shape[0]
    cols = rel_bias[:, REL_FIRST:]
    cols = jnp.pad(cols, ((0, 0), (0, REL_COLS - cols.shape[1])))
    tile = pl.pallas_call(
        _bias_kernel,
        out_shape=jax.ShapeDtypeStruct((BAND_Q, h, BAND_WIN), F32),
        compiler_params=pltpu.CompilerParams(vmem_limit_bytes=V7X_VMEM_LIMIT),
        name="band_bias_tile",
    )(cols)
    return jnp.transpose(tile, (1, 0, 2))


def _band_core(q_pair, kw, vw, bias0, bias1, valid, lane_lo):
    bq = q_pair.shape[0]
    z = _dot_t(_split_heads(q_pair, lane_lo), kw)
    s = z + jnp.concatenate([bias0, bias1], axis=0)
    s = jnp.where(valid, s, -jnp.inf)
    e = jnp.exp(s - jnp.max(s, axis=-1, keepdims=True))
    inv = 1.0 / jnp.sum(e, axis=-1, keepdims=True)
    e = e.astype(BF16)
    o = _dot(jnp.concatenate([e[:bq], e[bq:]], axis=1), _split_heads(vw, lane_lo))
    return (o * jnp.where(lane_lo, inv[:bq], inv[bq:])).astype(BF16)


def _band_prompt_kernel(q_ref, k_ref, v_ref, bias_ref, o_ref, kpad_ref, vpad_ref):
    i = pl.program_id(2)
    bq = q_ref.shape[1]
    npair = q_ref.shape[2] // V7X_LANES
    win = bias_ref.shape[2]
    past = win - bq

    @pl.when(i == 0)
    def _():
        kpad_ref[0:past, :] = jnp.zeros((past, kpad_ref.shape[1]), BF16)
        vpad_ref[0:past, :] = jnp.zeros((past, vpad_ref.shape[1]), BF16)
        kpad_ref[past:, :] = k_ref[0]
        vpad_ref[past:, :] = v_ref[0]

    rows = pl.ds(pl.multiple_of(i * bq, bq), win)
    valid = lax.broadcasted_iota(jnp.int32, (1, win), 1) >= past - i * bq
    lane_lo = _lane_lo()
    for p in range(npair):
        o_ref[0, :, _pair(p)] = _band_core(q_ref[0, :, _pair(p)], kpad_ref[rows, _pair(p)], vpad_ref[rows, _pair(p)],
                                           bias_ref[2 * p], bias_ref[2 * p + 1], valid, lane_lo)


def _band_prompt(q, k, v, bias):
    b, t, cw = q.shape
    bq, win = bias.shape[1], bias.shape[2]
    width = PAIRS_PER_STEP * V7X_LANES
    qspec = pl.BlockSpec((1, bq, width), lambda p, bi, i: (bi, i, p))
    kvspec = pl.BlockSpec((1, t, width), lambda p, bi, i: (bi, 0, p))
    return pl.pallas_call(
        _band_prompt_kernel,
        grid=(cw // width, b, t // bq),
        in_specs=[qspec, kvspec, kvspec, pl.BlockSpec((2 * PAIRS_PER_STEP, bq, win), lambda p, bi, i: (p, 0, 0))],
        out_specs=qspec,
        out_shape=jax.ShapeDtypeStruct((b, t, cw), BF16),
        scratch_shapes=[pltpu.VMEM((win - bq + t, width), BF16)] * 2,
        compiler_params=_params("arbitrary", "arbitrary", "arbitrary"),
        name="band_attn_prompt",
    )(q, k, v, bias)


def _band_sample_kernel(q_ref, kc_ref, vc_ref, kn_ref, vn_ref, bias_ref, o_ref, kpad_ref, vpad_ref):
    t = q_ref.shape[1]
    npair = q_ref.shape[2] // V7X_LANES
    past = kc_ref.shape[1]
    win = bias_ref.shape[2]
    kpad_ref[0:past, :] = kc_ref[0].astype(BF16)
    vpad_ref[0:past, :] = vc_ref[0].astype(BF16)
    kpad_ref[past:past + t, :] = kn_ref[0]
    vpad_ref[past:past + t, :] = vn_ref[0]
    kpad_ref[past + t:, :] = jnp.zeros((win - past - t, kpad_ref.shape[1]), BF16)
    vpad_ref[past + t:, :] = jnp.zeros((win - past - t, vpad_ref.shape[1]), BF16)
    valid = lax.broadcasted_iota(jnp.int32, (1, win), 1) < past + t
    lane_lo = _lane_lo()
    for p in range(npair):
        o_ref[0, :, _pair(p)] = _band_core(q_ref[0, :, _pair(p)], kpad_ref[:, _pair(p)], vpad_ref[:, _pair(p)],
                                           bias_ref[2 * p], bias_ref[2 * p + 1], valid, lane_lo)


def _band_sample(q, kc, vc, kn, vn, bias):
    b, t, cw = q.shape
    nh = bias.shape[0]
    past, win = kc.shape[1], bias.shape[2]
    new = pl.BlockSpec((1, t, cw), lambda bi: (bi, 0, 0))
    cache = pl.BlockSpec((1, past, cw), lambda bi: (bi, 0, 0))
    return pl.pallas_call(
        _band_sample_kernel,
        grid=(b,),
        in_specs=[new, cache, cache, new, new, pl.BlockSpec((nh, t, win), lambda bi: (0, 0, 0))],
        out_specs=new,
        out_shape=jax.ShapeDtypeStruct((b, t, cw), BF16),
        scratch_shapes=[pltpu.VMEM((win, cw), BF16)] * 2,
        compiler_params=_params("arbitrary"),
        name="band_attn_sample",
    )(q, kc, vc, kn, vn, bias)


def kernel(x_prompt, x_sample, cache_sb_k, cache_sb_v, cache_conv, cache_band_k, cache_band_v, norm_mix, norm_ffn,
           norm_final, w_in_ab, w_out_ab, dw_w, dw_b, conv_ln_g, conv_ln_b, w_in_c, w_out_c, rel_bias, w_up, w_down):
    b, t, d = x_prompt.shape
    bs, ts, _ = x_sample.shape
    n_sb, n_c = cache_sb_k.shape[3], cache_band_k.shape[3]
    sw, cc, cw = n_sb * HEAD_DIM, dw_w.shape[2], n_c * HEAD_DIM
    past = cache_sb_k.shape[2]
    band_past = cache_band_k.shape[2]
    keep = min(BAND_PAST, t)
    assert w_in_ab.shape[0] == 1 and w_in_c.shape[0] == 1 and norm_mix.shape[0] == 2
    assert sw == cc and sw + cc == d and cw == d and w_in_ab.shape[2] == 3 * sw + 2 * cc
    assert t % ROW_TILE == 0 and ts % 16 == 0 and ts <= V7X_MXU_DIM
    assert past % V7X_MXU_DIM == 0 and band_past == BAND_PAST and band_past + ts <= BAND_WIN

    tm_p, tm_s = ROW_TILE, bs * ts
    vec = lambda a: a.reshape(1, -1)
    w_in0, w_out0 = w_in_ab[0].astype(BF16), w_out_ab[0].astype(BF16)
    w_in1, w_out1 = w_in_c[0].astype(BF16), w_out_c[0].astype(BF16)
    w_up_b, w_down_b = w_up.astype(BF16), w_down.astype(BF16)
    dw_pad = jnp.pad(dw_w[0], ((0, HALO_ROWS - CONV_WIDTH), (0, 0)))
    conv_vecs = (vec(dw_b[0]), vec(conv_ln_g[0]), vec(conv_ln_b[0]))
    bias = _bias_tile(rel_bias[0])

    def layer0(x, tm, attn, conv):
        q, kb, vb, k, v, u = _ab_in(x, vec(norm_mix[0]), w_in0, sw, cc, tm)
        a = attn(q, kb, vb)
        c = conv(u)
        x2 = _out_mlp(x, a, c, 0, w_out0, vec(norm_ffn[0]), w_up_b[0], w_down_b[0], vec(norm_final), tm, False)
        return x2, k, v, u

    def layer1(x, tm, tlen, tail, attn):
        q, k, v, kt, vt = _c_in(x, vec(norm_mix[1]), w_in1, tlen, tail, tm)
        o = attn(q, k, v)
        y = _out_mlp(x, o, o, 1, w_out1, vec(norm_ffn[1]), w_up_b[1], w_down_b[1], vec(norm_final), tm, True)
        return y, kt, vt

    shp = lambda z, n: z.reshape(b, t, n)
    xp2, kp, vp, up = layer0(
        x_prompt.reshape(b * t, d), tm_p,
        lambda q, k, v: _sb_prompt(shp(q, sw), shp(k, sw), shp(v, sw)).reshape(b * t, sw),
        lambda u: _conv(shp(u, cc), shp(u, cc), dw_pad, *conv_vecs, CONV_TILE, False).reshape(b * t, cc))
    yp, bkp, bvp = layer1(
        xp2, tm_p, t, keep,
        lambda q, k, v: _band_prompt(shp(q, cw), shp(k, cw), shp(v, cw), bias).reshape(b * t, cw))

    shs = lambda z, n: z.reshape(bs, ts, n)
    conv_hist = jnp.pad(cache_conv[0], ((0, 0), (HALO_ROWS - CONV_STATE, 0), (0, 0)))
    ck, cv = cache_sb_k[0].reshape(bs, past, sw), cache_sb_v[0].reshape(bs, past, sw)
    xs2, ks, vs, us = layer0(
        x_sample.reshape(bs * ts, d), tm_s,
        lambda q, k, v: _sb_sample(shs(q, sw), ck, cv, shs(k, sw), shs(v, sw)).reshape(bs * ts, sw),
        lambda u: _conv(shs(u, cc), conv_hist, dw_pad, *conv_vecs, ts, True).reshape(bs * ts, cc))
    bck, bcv = cache_band_k[0].reshape(bs, band_past, cw), cache_band_v[0].reshape(bs, band_past, cw)
    ys, bks, bvs = layer1(
        xs2, tm_s, ts, ts,
        lambda q, k, v: _band_sample(shs(q, cw), bck, bcv, shs(k, cw), shs(v, cw), bias).reshape(bs * ts, cw))

    new_conv_p = up.reshape(b, t, cc)[:, t - CONV_STATE:]
    new_conv_s = jnp.concatenate([cache_conv[0], us.reshape(bs, ts, cc)], axis=1)[:, ts:]
    return (yp.reshape(b, t, d), ys.reshape(bs, ts, d),
            kp.reshape(1, b, t, n_sb, HEAD_DIM), vp.reshape(1, b, t, n_sb, HEAD_DIM), new_conv_p[None],
            bkp.reshape(1, b, keep, n_c, HEAD_DIM), bvp.reshape(1, b, keep, n_c, HEAD_DIM),
            ks.reshape(1, bs, ts, n_sb, HEAD_DIM), vs.reshape(1, bs, ts, n_sb, HEAD_DIM), new_conv_s[None],
            bks.reshape(1, bs, ts, n_c, HEAD_DIM), bvs.reshape(1, bs, ts, n_c, HEAD_DIM))
```

```python
import functools

import jax
import jax.numpy as jnp
from jax import lax
from jax.experimental import pallas as pl
from jax.experimental.pallas import tpu as pltpu

F32 = jnp.float32
BF16 = jnp.bfloat16

HEAD_DIM = 64
CHUNK = 64
CHUNK_SHIFT = 6
LEFT_CHUNKS = 8
BAND_PAST = LEFT_CHUNKS * CHUNK
REL_CLIP = 128
CONV_WIDTH = 31
CONV_STATE = CONV_WIDTH - 1
RMS_EPS = 1e-6
LN_EPS = 1e-5
LOG2E = 1.4426950408889634
SCALE = HEAD_DIM ** -0.5

V7X_LANES = 128
SUBLANES = 8
V7X_MXU_DIM = 256
V7X_VMEM_LIMIT = 56 * 1024 * 1024
ROW_TILE = 512
FF_CHUNK = 512
CONV_TILE = 256
HALO_ROWS = 32
CONV_ROWS = 64
PAIRS_PER_STEP = 4
BAND_Q = 256
BAND_WIN = BAND_PAST + BAND_Q
REL_FIRST = REL_CLIP + 1 - CHUNK
REL_COLS = 256
BIAS_UNROLL = 4


def _params(*semantics):
    return pltpu.CompilerParams(dimension_semantics=semantics, vmem_limit_bytes=V7X_VMEM_LIMIT)


def _dot(a, b):
    return jnp.dot(a, b, preferred_element_type=F32)


def _dot_t(a, b):
    return lax.dot_general(a, b, (((1,), (1,)), ((), ())), preferred_element_type=F32)


def _rms_rows(x, g):
    r = lax.rsqrt(jnp.mean(x * x, axis=-1, keepdims=True) + RMS_EPS)
    return (x * r) * g


def _resident(shape):
    return pl.BlockSpec(shape, lambda *_: (0,) * len(shape), pipeline_mode=pl.Buffered(1))


def _lane_lo():
    return lax.broadcasted_iota(jnp.int32, (1, V7X_LANES), 1) < HEAD_DIM


def _split_heads(xp, lane_lo):
    zero = jnp.zeros_like(xp)
    return jnp.concatenate([jnp.where(lane_lo, xp, zero), jnp.where(lane_lo, zero, xp)], axis=0)


def _pair(p):
    return slice(p * V7X_LANES, (p + 1) * V7X_LANES)


def _store_heads(dst_ref, src_ref):
    nh = dst_ref.shape[1]
    per_head = jnp.stack([src_ref[:, h * HEAD_DIM:(h + 1) * HEAD_DIM] for h in range(nh)], axis=0)
    dst_ref[...] = pltpu.einshape("htd->thd", per_head)


def _ab_in_kernel(x_ref, g_ref, w_ref, q_ref, kb_ref, vb_ref, k_ref, v_ref, u_ref, h_ref, kf_ref, vf_ref):
    h_ref[...] = _rms_rows(x_ref[...], g_ref[...]).astype(BF16)
    sw = q_ref.shape[-1]
    cw = u_ref.shape[-1]
    proj = lambda c0, n: _dot(h_ref[...], w_ref[:, c0:c0 + n])
    q_ref[...] = (proj(0, sw) * SCALE).astype(BF16)
    for c0, b_ref, o_ref, f_ref in ((sw, kb_ref, k_ref, kf_ref), (2 * sw, vb_ref, v_ref, vf_ref)):
        f_ref[...] = proj(c0, sw)
        b_ref[...] = f_ref[...].astype(BF16)
        _store_heads(o_ref, f_ref)
    u_ref[...] = proj(3 * sw, cw) * jax.nn.sigmoid(proj(3 * sw + cw, cw))


def _ab_in(x, g, w, sw, cw, tm):
    m, d = x.shape
    nh = sw // HEAD_DIM
    row = lambda n: pl.BlockSpec((tm, n), lambda i: (i, 0))
    heads = pl.BlockSpec((tm, nh, HEAD_DIM), lambda i: (i, 0, 0))
    return pl.pallas_call(
        _ab_in_kernel,
        grid=(m // tm,),
        in_specs=[row(d), _resident((1, d)), _resident(w.shape)],
        out_specs=[row(sw)] * 3 + [heads] * 2 + [row(cw)],
        out_shape=[jax.ShapeDtypeStruct((m, sw), BF16)] * 3 + [jax.ShapeDtypeStruct((m, nh, HEAD_DIM), F32)] * 2
        + [jax.ShapeDtypeStruct((m, cw), F32)],
        scratch_shapes=[pltpu.VMEM((tm, d), BF16)] + [pltpu.VMEM((tm, sw), F32)] * 2,
        compiler_params=_params("arbitrary"),
        name="ab_in_proj",
    )(x, g, w)


def _softplus(z):
    return jnp.maximum(z, 0.0) + jnp.log(1.0 + jnp.exp2(jnp.abs(z) * -LOG2E))


def _sb_scores(q2s, kblks):
    return [_dot_t(q2, kblk) for q2, kblk in zip(q2s, kblks)]


def _sb_cumsums(zs, tri2, mask):
    cs = []
    for z in zs:
        sp = _softplus(z)
        if mask is not None:
            sp = jnp.where(mask, sp, 0.0)
        hi = sp.astype(BF16)
        lo = (sp - hi.astype(F32)).astype(BF16)
        cs.append(_dot(jnp.concatenate([hi, lo], axis=1), tri2))
    return cs


def _sb_weigh(zs, cs, vblks, states, mask, lane_lo):
    out = []
    for z, c, vblk, (carry, acc) in zip(zs, cs, vblks, states):
        bq = acc.shape[0]
        w = jnp.exp2((z - c - carry) * LOG2E)
        if mask is not None:
            w = jnp.where(mask, w, 0.0)
        w = w.astype(BF16)
        wcat = jnp.concatenate([w[:bq], w[bq:]], axis=1)
        out.append((carry + c[:, 0:1], acc + _dot(wcat, _split_heads(vblk, lane_lo))))
    return tuple(out)


def _sb_blocks(q2s, kblks, vblks, tri2, states, mask, lane_lo):
    zs = _sb_scores(q2s, kblks)
    return _sb_weigh(zs, _sb_cumsums(zs, tri2, mask), vblks, states, mask, lane_lo)


def _sb_consts(bq, kb):
    tri = jnp.where((lax.broadcasted_iota(jnp.int32, (2 * kb, kb), 0) & (kb - 1))
                    >= lax.broadcasted_iota(jnp.int32, (2 * kb, kb), 1), 1.0, 0.0).astype(BF16)
    assert bq & (bq - 1) == 0
    t_loc = lax.broadcasted_iota(jnp.int32, (2 * bq, kb), 0) & (bq - 1)
    causal = lax.broadcasted_iota(jnp.int32, (2 * bq, kb), 1) < t_loc
    return _lane_lo(), tri, causal


def _sb_zero_state(bq):
    return jnp.zeros((2 * bq, 1), F32), jnp.zeros((bq, V7X_LANES), F32)


def _sb_prompt_kernel(q_ref, k_ref, v_ref, o_ref):
    i = pl.program_id(2)
    bq = q_ref.shape[1]
    npair = q_ref.shape[2] // V7X_LANES
    lane_lo, tri, causal = _sb_consts(bq, bq)
    q2 = [_split_heads(q_ref[0, :, _pair(p)], lane_lo) for p in range(npair)]

    def blocks(j):
        rows = pl.ds(pl.multiple_of(j * bq, bq), bq)
        return ([k_ref[0, rows, _pair(p)] for p in range(npair)], [v_ref[0, rows, _pair(p)] for p in range(npair)])

    def sweep(j, states, mask):
        return _sb_blocks(q2, *blocks(j), tri, states, mask, lane_lo)

    def sweep2(s, states):
        j = i - 1 - 2 * s
        (ka, va), (kb, vb) = blocks(j), blocks(j - 1)
        z_a = _sb_scores(q2, ka)
        z_b = _sb_scores(q2, kb)
        c_a = _sb_cumsums(z_a, tri, None)
        c_b = _sb_cumsums(z_b, tri, None)
        states = _sb_weigh(z_a, c_a, va, states, None, lane_lo)
        return _sb_weigh(z_b, c_b, vb, states, None, lane_lo)

    states = sweep(i, (_sb_zero_state(bq),) * npair, causal)
    states = lax.fori_loop(0, i // 2, sweep2, states)
    states = lax.cond(i % 2 == 1, lambda st: sweep(0, st, None), lambda st: st, states)
    for p in range(npair):
        o_ref[0, :, _pair(p)] = states[p][1].astype(BF16)


def _sb_prompt(q, k, v):
    b, t, sw = q.shape
    bq = V7X_MXU_DIM
    width = PAIRS_PER_STEP * V7X_LANES
    qspec = pl.BlockSpec((1, bq, width), lambda bi, p, i: (bi, i, p))
    kvspec = pl.BlockSpec((1, t, width), lambda bi, p, i: (bi, 0, p))
    return pl.pallas_call(
        _sb_prompt_kernel,
        grid=(b, sw // width, t // bq),
        in_specs=[qspec, kvspec, kvspec],
        out_specs=qspec,
        out_shape=jax.ShapeDtypeStruct((b, t, sw), BF16),
        compiler_params=_params("arbitrary", "arbitrary", "arbitrary"),
        name="sb_attn_prompt",
    )(q, k, v)


def _sb_sample_kernel(q_ref, kc_ref, vc_ref, kn_ref, vn_ref, o_ref, kpad_ref, vpad_ref):
    bq = q_ref.shape[1]
    kb = kpad_ref.shape[0]
    npair = q_ref.shape[2] // V7X_LANES
    nblk = kc_ref.shape[1] // kb
    lane_lo, tri2, causal = _sb_consts(bq, kb)
    kpad_ref[...] = jnp.zeros_like(kpad_ref)
    vpad_ref[...] = jnp.zeros_like(vpad_ref)
    kpad_ref[0:bq, :] = kn_ref[0]
    vpad_ref[0:bq, :] = vn_ref[0]
    for p in range(npair):
        q2 = _split_heads(q_ref[0, :, _pair(p)], lane_lo)
        cache_rows = [slice(j * kb, (j + 1) * kb) for j in reversed(range(nblk))]
        kblocks = [kpad_ref[:, _pair(p)]] + [kc_ref[0, r, _pair(p)].astype(BF16) for r in cache_rows]
        vblocks = [vpad_ref[:, _pair(p)]] + [vc_ref[0, r, _pair(p)].astype(BF16) for r in cache_rows]
        z = [_dot_t(q2, kblk) for kblk in kblocks]
        sp = [_softplus(zb) for zb in z]
        sp[0] = jnp.where(causal, sp[0], 0.0)
        sp_all = jnp.concatenate(sp, axis=0)
        hi = sp_all.astype(BF16)
        lo = (sp_all - hi.astype(F32)).astype(BF16)
        c_all = _dot(jnp.concatenate([hi, lo], axis=1), tri2)
        carry = jnp.zeros((2 * bq, 1), F32)
        ws = []
        for j, zb in enumerate(z):
            c = c_all[j * 2 * bq:(j + 1) * 2 * bq]
            w = jnp.exp2((zb - c - carry) * LOG2E)
            if j == 0:
                w = jnp.where(causal, w, 0.0)
            carry = carry + c[:, 0:1]
            w = w.astype(BF16)
            ws.append(jnp.concatenate([w[:bq], w[bq:]], axis=1))
        vcat = jnp.concatenate([_split_heads(vblk, lane_lo) for vblk in vblocks], axis=0)
        o_ref[0, :, _pair(p)] = _dot(jnp.concatenate(ws, axis=1), vcat).astype(BF16)


def _sb_sample(q, kc, vc, kn, vn):
    b, t, sw = q.shape
    past = kc.shape[1]
    new = pl.BlockSpec((1, t, sw), lambda bi: (bi, 0, 0))
    cache = pl.BlockSpec((1, past, sw), lambda bi: (bi, 0, 0))
    return pl.pallas_call(
        _sb_sample_kernel,
        grid=(b,),
        in_specs=[new, cache, cache, new, new],
        out_specs=new,
        out_shape=jax.ShapeDtypeStruct((b, t, sw), BF16),
        scratch_shapes=[pltpu.VMEM((V7X_MXU_DIM, sw), BF16)] * 2,
        compiler_params=_params("arbitrary"),
        name="sb_attn_sample",
    )(q, kc, vc, kn, vn)


def _conv_kernel(u_ref, halo_ref, dw_ref, b_ref, g_ref, be_ref, c_ref, ext_ref, sh_ref, *, halo_is_history):
    tm = u_ref.shape[1]
    halo = halo_ref[0]
    if not halo_is_history:
        halo = jnp.where(pl.program_id(1) > 0, halo, 0.0)
    ext_ref[0:HALO_ROWS, :] = halo
    ext_ref[HALO_ROWS:, :] = u_ref[0]
    n = sh_ref.shape[1]
    for s in range(1, SUBLANES):
        sh_ref[s - 1] = ext_ref[s:s + n, :]
    first = HALO_ROWS - CONV_STATE
    rows = min(CONV_ROWS, tm)
    for r0 in range(0, tm, rows):
        y = b_ref[...]
        for w in range(CONV_WIDTH):
            s = (first + w) % SUBLANES
            a = r0 + first + w - s
            tap = ext_ref[a:a + rows, :] if s == 0 else sh_ref[s - 1, a:a + rows, :]
            y = y + tap * dw_ref[w:w + 1, :]
        mu = jnp.mean(y, axis=-1, keepdims=True)
        var = jnp.mean(jnp.square(y - mu), axis=-1, keepdims=True)
        yn = ((y - mu) * lax.rsqrt(var + LN_EPS)) * g_ref[...] + be_ref[...]
        c_ref[0, r0:r0 + rows, :] = (yn * jax.nn.sigmoid(yn)).astype(BF16)


def _conv(u, halo_src, dw, b, g, be, tm, halo_is_history):
    bsz, t, c = u.shape
    if halo_is_history:
        halo_spec = pl.BlockSpec((1, HALO_ROWS, c), lambda bi, ti: (bi, 0, 0))
    else:
        per = tm // HALO_ROWS
        halo_spec = pl.BlockSpec((1, HALO_ROWS, c), lambda bi, ti: (bi, jnp.maximum(ti * per - 1, 0), 0))
    tile = pl.BlockSpec((1, tm, c), lambda bi, ti: (bi, ti, 0))
    vec = _resident((1, c))
    return pl.pallas_call(
        functools.partial(_conv_kernel, halo_is_history=halo_is_history),
        grid=(bsz, t // tm),
        in_specs=[tile, halo_spec, _resident(dw.shape), vec, vec, vec],
        out_specs=tile,
        out_shape=jax.ShapeDtypeStruct((bsz, t, c), BF16),
        scratch_shapes=[pltpu.VMEM((HALO_ROWS + tm, c), F32),
                        pltpu.VMEM((SUBLANES - 1, HALO_ROWS + tm - SUBLANES, c), F32)],
        compiler_params=_params("arbitrary", "arbitrary"),
        name="conv_module",
    )(u, halo_src, dw, b, g, be)


def _out_mlp_kernel(x_ref, m0_ref, m1_ref, wo_ref, g_ref, wu_ref, wd_ref, gf_ref, y_ref, h_ref, *, final_norm):
    half = m0_ref.shape[-1]
    x1 = x_ref[...] + _dot(m0_ref[...], wo_ref[0:half, :]) + _dot(m1_ref[...], wo_ref[half:, :])
    h_ref[...] = _rms_rows(x1, g_ref[...]).astype(BF16)
    y_ref[...] = x1
    for c0 in range(0, wu_ref.shape[1], FF_CHUNK):
        up = _dot(h_ref[...], wu_ref[:, c0:c0 + FF_CHUNK])
        act = jnp.square(jnp.maximum(up, 0.0)).astype(BF16)
        y_ref[...] += _dot(act, wd_ref[c0:c0 + FF_CHUNK, :])
    if final_norm:
        y_ref[...] = _rms_rows(y_ref[...], gf_ref[...])


def _out_mlp(x, m0, m1, m1_block, wo, g, wu, wd, gf, tm, final_norm):
    m, d = x.shape
    half = wo.shape[0] // 2
    row = pl.BlockSpec((tm, d), lambda i: (i, 0))
    return pl.pallas_call(
        functools.partial(_out_mlp_kernel, final_norm=final_norm),
        grid=(m // tm,),
        in_specs=[row, pl.BlockSpec((tm, half), lambda i: (i, 0)), pl.BlockSpec((tm, half), lambda i: (i, m1_block)),
                  _resident(wo.shape), _resident((1, d)), _resident(wu.shape), _resident(wd.shape), _resident((1, d))],
        out_specs=row,
        out_shape=jax.ShapeDtypeStruct((m, d), F32),
        scratch_shapes=[pltpu.VMEM((tm, d), BF16)],
        compiler_params=_params("arbitrary"),
        name="out_proj_mlp",
    )(x, m0, m1, wo, g, wu, wd, gf)


def _c_in_kernel(x_ref, g_ref, w_ref, q_ref, k_ref, v_ref, kt_ref, vt_ref, h_ref, kf_ref, vf_ref, *, tail_every):
    h_ref[...] = _rms_rows(x_ref[...], g_ref[...]).astype(BF16)
    cw = q_ref.shape[-1]
    half = cw // 2
    for c0 in range(0, cw, half):
        cols = slice(c0, c0 + half)
        q_ref[:, cols] = (_dot(h_ref[...], w_ref[:, c0:c0 + half]) * SCALE).astype(BF16)
        for base, b_ref, f_ref in ((cw, k_ref, kf_ref), (2 * cw, v_ref, vf_ref)):
            f_ref[:, cols] = _dot(h_ref[...], w_ref[:, base + c0:base + c0 + half])
            b_ref[:, cols] = f_ref[:, cols].astype(BF16)

    @pl.when((pl.program_id(0) + 1) % tail_every == 0)
    def _():
        _store_heads(kt_ref, kf_ref)
        _store_heads(vt_ref, vf_ref)


def _c_in(x, g, w, t, keep, tm):
    m, d = x.shape
    cw = w.shape[1] // 3
    nh = cw // HEAD_DIM
    row = lambda n: pl.BlockSpec((tm, n), lambda i: (i, 0))
    if keep == t:
        tail_every = 1
        tail_map = lambda i: (i, 0, 0)
    else:
        assert keep == tm and t % tm == 0
        tail_every = t // tm
        tail_map = lambda i: (i // tail_every, 0, 0)
    tail = pl.BlockSpec((tm, nh, HEAD_DIM), tail_map)
    nb = m // t
    return pl.pallas_call(
        functools.partial(_c_in_kernel, tail_every=tail_every),
        grid=(m // tm,),
        in_specs=[row(d), _resident((1, d)), _resident(w.shape)],
        out_specs=[row(cw), row(cw), row(cw), tail, tail],
        out_shape=[jax.ShapeDtypeStruct((m, cw), BF16)] * 3 + [jax.ShapeDtypeStruct((nb * keep, nh, HEAD_DIM), F32)] * 2,
        scratch_shapes=[pltpu.VMEM((tm, d), BF16)] + [pltpu.VMEM((tm, cw), F32)] * 2,
        compiler_params=_params("arbitrary"),
        name="c_in_proj",
    )(x, g, w)


def _bias_kernel(rb_ref, o_ref):
    rb = rb_ref[...]
    nh = rb.shape[0]
    hi = rb.astype(BF16)
    r1 = rb - hi.astype(F32)
    mid = r1.astype(BF16)
    lo = (r1 - mid.astype(F32)).astype(BF16)
    terms = jnp.concatenate([hi, mid, lo], axis=0)
    ncol, win = rb.shape[1], o_ref.shape[2]
    s = lax.broadcasted_iota(jnp.int32, (1, win), 1)
    kk = lax.broadcasted_iota(jnp.int32, (ncol, win), 0)

    def body(r, carry):
        idx = jnp.clip(BAND_PAST + r - s, -REL_CLIP, REL_CLIP) + REL_CLIP - REL_FIRST
        kc, qc = s >> CHUNK_SHIFT, r >> CHUNK_SHIFT
        visible = (kc >= qc) & (kc <= qc + LEFT_CHUNKS)
        onehot = jnp.where((kk == idx) & visible, 1.0, 0.0).astype(BF16)
        parts = _dot(terms, onehot)
        row = (parts[0:nh] + parts[nh:2 * nh]) + parts[2 * nh:3 * nh]
        o_ref[r] = jnp.where(visible, row, -jnp.inf)
        return carry

    lax.fori_loop(0, o_ref.shape[0], body, 0, unroll=BIAS_UNROLL)


def _bias_tile(rel_bias):
    h = rel_bias.shape[0]
    cols = rel_bias[:, REL_FIRST:]
    cols = jnp.pad(cols, ((0, 0), (0, REL_COLS - cols.shape[1])))
    tile = pl.pallas_call(
        _bias_kernel,
        out_shape=jax.ShapeDtypeStruct((BAND_Q, h, BAND_WIN), F32),
        compiler_params=pltpu.CompilerParams(vmem_limit_bytes=V7X_VMEM_LIMIT),
        name="band_bias_tile",
    )(cols)
    return jnp.transpose(tile, (1, 0, 2))


def _band_core(q_pairs, kws, vws, biases, valid, lane_lo):
    bq = q_pairs[0].shape[0]
    zs = [_dot_t(_split_heads(q_pair, lane_lo), kw) for q_pair, kw in zip(q_pairs, kws)]
    es, invs = [], []
    for z, (bias0, bias1) in zip(zs, biases):
        s = z + jnp.concatenate([bias0, bias1], axis=0)
        s = jnp.where(valid, s, -jnp.inf)
        e = jnp.exp2((s - jnp.max(s, axis=-1, keepdims=True)) * LOG2E)
        invs.append(1.0 / jnp.sum(e, axis=-1, keepdims=True))
        es.append(e.astype(BF16))
    outs = []
    for e, inv, vw in zip(es, invs, vws):
        o = _dot(jnp.concatenate([e[:bq], e[bq:]], axis=1), _split_heads(vw, lane_lo))
        outs.append((o * jnp.where(lane_lo, inv[:bq], inv[bq:])).astype(BF16))
    return outs


def _band_prompt_kernel(q_ref, k_ref, v_ref, bias_ref, o_ref, kpad_ref, vpad_ref):
    i = pl.program_id(2)
    bq = q_ref.shape[1]
    npair = q_ref.shape[2] // V7X_LANES
    win = bias_ref.shape[2]
    past = win - bq

    @pl.when(i == 0)
    def _():
        kpad_ref[0:past, :] = jnp.zeros((past, kpad_ref.shape[1]), BF16)
        vpad_ref[0:past, :] = jnp.zeros((past, vpad_ref.shape[1]), BF16)
        kpad_ref[past:, :] = k_ref[0]
        vpad_ref[past:, :] = v_ref[0]

    rows = pl.ds(pl.multiple_of(i * bq, bq), win)
    valid = lax.broadcasted_iota(jnp.int32, (1, win), 1) >= past - i * bq
    lane_lo = _lane_lo()
    pairs = range(npair)
    outs = _band_core([q_ref[0, :, _pair(p)] for p in pairs], [kpad_ref[rows, _pair(p)] for p in pairs],
                      [vpad_ref[rows, _pair(p)] for p in pairs], [(bias_ref[2 * p], bias_ref[2 * p + 1]) for p in pairs],
                      valid, lane_lo)
    for p in pairs:
        o_ref[0, :, _pair(p)] = outs[p]


def _band_prompt(q, k, v, bias):
    b, t, cw = q.shape
    bq, win = bias.shape[1], bias.shape[2]
    width = PAIRS_PER_STEP * V7X_LANES
    qspec = pl.BlockSpec((1, bq, width), lambda p, bi, i: (bi, i, p))
    kvspec = pl.BlockSpec((1, t, width), lambda p, bi, i: (bi, 0, p))
    return pl.pallas_call(
        _band_prompt_kernel,
        grid=(cw // width, b, t // bq),
        in_specs=[qspec, kvspec, kvspec, pl.BlockSpec((2 * PAIRS_PER_STEP, bq, win), lambda p, bi, i: (p, 0, 0))],
        out_specs=qspec,
        out_shape=jax.ShapeDtypeStruct((b, t, cw), BF16),
        scratch_shapes=[pltpu.VMEM((win - bq + t, width), BF16)] * 2,
        compiler_params=_params("arbitrary", "arbitrary", "arbitrary"),
        name="band_attn_prompt",
    )(q, k, v, bias)


def _band_sample_kernel(q_ref, kc_ref, vc_ref, kn_ref, vn_ref, bias_ref, o_ref, kpad_ref, vpad_ref):
    t = q_ref.shape[1]
    npair = q_ref.shape[2] // V7X_LANES
    past = kc_ref.shape[1]
    win = bias_ref.shape[2]
    kpad_ref[0:past, :] = kc_ref[0].astype(BF16)
    vpad_ref[0:past, :] = vc_ref[0].astype(BF16)
    kpad_ref[past:past + t, :] = kn_ref[0]
    vpad_ref[past:past + t, :] = vn_ref[0]
    kpad_ref[past + t:, :] = jnp.zeros((win - past - t, kpad_ref.shape[1]), BF16)
    vpad_ref[past + t:, :] = jnp.zeros((win - past - t, vpad_ref.shape[1]), BF16)
    valid = lax.broadcasted_iota(jnp.int32, (1, win), 1) < past + t
    lane_lo = _lane_lo()
    pairs = range(npair)
    outs = _band_core([q_ref[0, :, _pair(p)] for p in pairs], [kpad_ref[:, _pair(p)] for p in pairs],
                      [vpad_ref[:, _pair(p)] for p in pairs], [(bias_ref[2 * p], bias_ref[2 * p + 1]) for p in pairs],
                      valid, lane_lo)
    for p in pairs:
        o_ref[0, :, _pair(p)] = outs[p]


def _band_sample(q, kc, vc, kn, vn, bias):
    b, t, cw = q.shape
    nh = bias.shape[0]
    past, win = kc.shape[1], bias.shape[2]
    new = pl.BlockSpec((1, t, cw), lambda bi: (bi, 0, 0))
    cache = pl.BlockSpec((1, past, cw), lambda bi: (bi, 0, 0))
    return pl.pallas_call(
        _band_sample_kernel,
        grid=(b,),
        in_specs=[new, cache, cache, new, new, pl.BlockSpec((nh, t, win), lambda bi: (0, 0, 0))],
        out_specs=new,
        out_shape=jax.ShapeDtypeStruct((b, t, cw), BF16),
        scratch_shapes=[pltpu.VMEM((win, cw), BF16)] * 2,
        compiler_params=_params("arbitrary"),
        name="band_attn_sample",
    )(q, kc, vc, kn, vn, bias)


def kernel(x_prompt, x_sample, cache_sb_k, cache_sb_v, cache_conv, cache_band_k, cache_band_v, norm_mix, norm_ffn,
           norm_final, w_in_ab, w_out_ab, dw_w, dw_b, conv_ln_g, conv_ln_b, w_in_c, w_out_c, rel_bias, w_up, w_down):
    b, t, d = x_prompt.shape
    bs, ts, _ = x_sample.shape
    n_sb, n_c = cache_sb_k.shape[3], cache_band_k.shape[3]
    sw, cc, cw = n_sb * HEAD_DIM, dw_w.shape[2], n_c * HEAD_DIM
    past = cache_sb_k.shape[2]
    band_past = cache_band_k.shape[2]
    keep = min(BAND_PAST, t)
    assert w_in_ab.shape[0] == 1 and w_in_c.shape[0] == 1 and norm_mix.shape[0] == 2
    assert sw == cc and sw + cc == d and cw == d and w_in_ab.shape[2] == 3 * sw + 2 * cc
    assert t % ROW_TILE == 0 and ts % 16 == 0 and ts <= V7X_MXU_DIM
    assert past % V7X_MXU_DIM == 0 and band_past == BAND_PAST and band_past + ts <= BAND_WIN

    tm_p, tm_s = ROW_TILE, bs * ts
    vec = lambda a: a.reshape(1, -1)
    w_in0, w_out0 = w_in_ab[0].astype(BF16), w_out_ab[0].astype(BF16)
    w_in1, w_out1 = w_in_c[0].astype(BF16), w_out_c[0].astype(BF16)
    w_up_b, w_down_b = w_up.astype(BF16), w_down.astype(BF16)
    dw_pad = jnp.pad(dw_w[0], ((0, HALO_ROWS - CONV_WIDTH), (0, 0)))
    conv_vecs = (vec(dw_b[0]), vec(conv_ln_g[0]), vec(conv_ln_b[0]))
    bias = _bias_tile(rel_bias[0])

    def layer0(x, tm, attn, conv):
        q, kb, vb, k, v, u = _ab_in(x, vec(norm_mix[0]), w_in0, sw, cc, tm)
        a = attn(q, kb, vb)
        c = conv(u)
        x2 = _out_mlp(x, a, c, 0, w_out0, vec(norm_ffn[0]), w_up_b[0], w_down_b[0], vec(norm_final), tm, False)
        return x2, k, v, u

    def layer1(x, tm, tlen, tail, attn):
        q, k, v, kt, vt = _c_in(x, vec(norm_mix[1]), w_in1, tlen, tail, tm)
        o = attn(q, k, v)
        y = _out_mlp(x, o, o, 1, w_out1, vec(norm_ffn[1]), w_up_b[1], w_down_b[1], vec(norm_final), tm, True)
        return y, kt, vt

    shp = lambda z, n: z.reshape(b, t, n)
    xp2, kp, vp, up = layer0(
        x_prompt.reshape(b * t, d), tm_p,
        lambda q, k, v: _sb_prompt(shp(q, sw), shp(k, sw), shp(v, sw)).reshape(b * t, sw),
        lambda u: _conv(shp(u, cc), shp(u, cc), dw_pad, *conv_vecs, CONV_TILE, False).reshape(b * t, cc))
    yp, bkp, bvp = layer1(
        xp2, tm_p, t, keep,
        lambda q, k, v: _band_prompt(shp(q, cw), shp(k, cw), shp(v, cw), bias).reshape(b * t, cw))

    shs = lambda z, n: z.reshape(bs, ts, n)
    conv_hist = jnp.pad(cache_conv[0], ((0, 0), (HALO_ROWS - CONV_STATE, 0), (0, 0)))
    ck, cv = cache_sb_k[0].reshape(bs, past, sw), cache_sb_v[0].reshape(bs, past, sw)
    xs2, ks, vs, us = layer0(
        x_sample.reshape(bs * ts, d), tm_s,
        lambda q, k, v: _sb_sample(shs(q, sw), ck, cv, shs(k, sw), shs(v, sw)).reshape(bs * ts, sw),
        lambda u: _conv(shs(u, cc), conv_hist, dw_pad, *conv_vecs, ts, True).reshape(bs * ts, cc))
    bck, bcv = cache_band_k[0].reshape(bs, band_past, cw), cache_band_v[0].reshape(bs, band_past, cw)
    ys, bks, bvs = layer1(
        xs2, tm_s, ts, ts,
        lambda q, k, v: _band_sample(shs(q, cw), bck, bcv, shs(k, cw), shs(v, cw), bias).reshape(bs * ts, cw))

    new_conv_p = up.reshape(b, t, cc)[:, t - CONV_STATE:]
    new_conv_s = jnp.concatenate([cache_conv[0], us.reshape(bs, ts, cc)], axis=1)[:, ts:]
    return (yp.reshape(b, t, d), ys.reshape(bs, ts, d),
            kp.reshape(1, b, t, n_sb, HEAD_DIM), vp.reshape(1, b, t, n_sb, HEAD_DIM), new_conv_p[None],
            bkp.reshape(1, b, keep, n_c, HEAD_DIM), bvp.reshape(1, b, keep, n_c, HEAD_DIM),
            ks.reshape(1, bs, ts, n_sb, HEAD_DIM), vs.reshape(1, bs, ts, n_sb, HEAD_DIM), new_conv_s[None],
            bks.reshape(1, bs, ts, n_c, HEAD_DIM), bvs.reshape(1, bs, ts, n_c, HEAD_DIM))
```

```python
import functools

import jax
import jax.numpy as jnp
from jax import lax
from jax.experimental import pallas as pl
from jax.experimental.pallas import tpu as pltpu

F32 = jnp.float32
BF16 = jnp.bfloat16

HEAD_DIM = 64
CHUNK = 64
CHUNK_SHIFT = 6
LEFT_CHUNKS = 8
BAND_PAST = LEFT_CHUNKS * CHUNK
REL_CLIP = 128
CONV_WIDTH = 31
CONV_STATE = CONV_WIDTH - 1
RMS_EPS = 1e-6
LN_EPS = 1e-5
LOG2E = 1.4426950408889634
SCALE = HEAD_DIM ** -0.5

V7X_LANES = 128
SUBLANES = 8
V7X_MXU_DIM = 256
V7X_VMEM_LIMIT = 56 * 1024 * 1024
ROW_TILE = 512
FF_CHUNK = 512
CONV_TILE = 256
HALO_ROWS = 32
CONV_ROWS = 64
PAIRS_PER_STEP = 4
BAND_Q = 256
BAND_WIN = BAND_PAST + BAND_Q
REL_FIRST = REL_CLIP + 1 - CHUNK
REL_COLS = 256
BIAS_UNROLL = 4


def _params(*semantics):
    return pltpu.CompilerParams(dimension_semantics=semantics, vmem_limit_bytes=V7X_VMEM_LIMIT)


def _dot(a, b):
    return jnp.dot(a, b, preferred_element_type=F32)


def _dot_t(a, b):
    return lax.dot_general(a, b, (((1,), (1,)), ((), ())), preferred_element_type=F32)


def _rms_rows(x, g):
    r = lax.rsqrt(jnp.mean(x * x, axis=-1, keepdims=True) + RMS_EPS)
    return (x * r) * g


def _resident(shape):
    return pl.BlockSpec(shape, lambda *_: (0,) * len(shape), pipeline_mode=pl.Buffered(1))


def _lane_lo():
    return lax.broadcasted_iota(jnp.int32, (1, V7X_LANES), 1) < HEAD_DIM


def _split_heads(xp, lane_lo):
    zero = jnp.zeros_like(xp)
    return jnp.concatenate([jnp.where(lane_lo, xp, zero), jnp.where(lane_lo, zero, xp)], axis=0)


def _pair(p):
    return slice(p * V7X_LANES, (p + 1) * V7X_LANES)


def _store_heads(dst_ref, src_ref):
    nh = dst_ref.shape[1]
    per_head = jnp.stack([src_ref[:, h * HEAD_DIM:(h + 1) * HEAD_DIM] for h in range(nh)], axis=0)
    dst_ref[...] = jnp.swapaxes(per_head, 0, 1)


def _load_pairs(src_ref, rows):
    per_head = jnp.swapaxes(src_ref[0, rows, :, :], 0, 1)
    return [jnp.concatenate([per_head[2 * p], per_head[2 * p + 1]], axis=-1).astype(BF16)
            for p in range(per_head.shape[0] // 2)]


def _ab_in_kernel(x_ref, g_ref, w_ref, q_ref, kb_ref, vb_ref, k_ref, v_ref, u_ref, h_ref, kf_ref, vf_ref):
    h_ref[...] = _rms_rows(x_ref[...], g_ref[...]).astype(BF16)
    sw = q_ref.shape[-1]
    cw = u_ref.shape[-1]
    proj = lambda c0, n: _dot(h_ref[...], w_ref[:, c0:c0 + n])
    q_ref[...] = (proj(0, sw) * SCALE).astype(BF16)
    for c0, b_ref, o_ref, f_ref in ((sw, kb_ref, k_ref, kf_ref), (2 * sw, vb_ref, v_ref, vf_ref)):
        f_ref[...] = proj(c0, sw)
        b_ref[...] = f_ref[...].astype(BF16)
        _store_heads(o_ref, f_ref)
    u_ref[...] = proj(3 * sw, cw) * jax.nn.sigmoid(proj(3 * sw + cw, cw))


def _ab_in(x, g, w, sw, cw, tm):
    m, d = x.shape
    nh = sw // HEAD_DIM
    row = lambda n: pl.BlockSpec((tm, n), lambda i: (i, 0))
    heads = pl.BlockSpec((tm, nh, HEAD_DIM), lambda i: (i, 0, 0))
    return pl.pallas_call(
        _ab_in_kernel,
        grid=(m // tm,),
        in_specs=[row(d), _resident((1, d)), _resident(w.shape)],
        out_specs=[row(sw)] * 3 + [heads] * 2 + [row(cw)],
        out_shape=[jax.ShapeDtypeStruct((m, sw), BF16)] * 3 + [jax.ShapeDtypeStruct((m, nh, HEAD_DIM), F32)] * 2
        + [jax.ShapeDtypeStruct((m, cw), F32)],
        scratch_shapes=[pltpu.VMEM((tm, d), BF16)] + [pltpu.VMEM((tm, sw), F32)] * 2,
        compiler_params=_params("arbitrary"),
        name="ab_in_proj",
    )(x, g, w)


def _softplus(z):
    return jnp.maximum(z, 0.0) + jnp.log(1.0 + jnp.exp2(jnp.abs(z) * -LOG2E))


def _sb_scores(q2s, kblks):
    return [_dot_t(q2, kblk) for q2, kblk in zip(q2s, kblks)]


def _sb_cumsums(zs, tri2, mask):
    cs = []
    for z in zs:
        sp = _softplus(z)
        if mask is not None:
            sp = jnp.where(mask, sp, 0.0)
        hi = sp.astype(BF16)
        lo = (sp - hi.astype(F32)).astype(BF16)
        cs.append(_dot(jnp.concatenate([hi, lo], axis=1), tri2))
    return cs


def _sb_weigh(zs, cs, vblks, states, mask, lane_lo):
    out = []
    for z, c, vblk, (carry, acc) in zip(zs, cs, vblks, states):
        bq = acc.shape[0]
        w = jnp.exp2((z - c - carry) * LOG2E)
        if mask is not None:
            w = jnp.where(mask, w, 0.0)
        w = w.astype(BF16)
        wcat = jnp.concatenate([w[:bq], w[bq:]], axis=1)
        out.append((carry + c[:, 0:1], acc + _dot(wcat, _split_heads(vblk, lane_lo))))
    return tuple(out)


def _sb_blocks(q2s, kblks, vblks, tri2, states, mask, lane_lo):
    zs = _sb_scores(q2s, kblks)
    return _sb_weigh(zs, _sb_cumsums(zs, tri2, mask), vblks, states, mask, lane_lo)


def _sb_consts(bq, kb):
    tri = jnp.where((lax.broadcasted_iota(jnp.int32, (2 * kb, kb), 0) & (kb - 1))
                    >= lax.broadcasted_iota(jnp.int32, (2 * kb, kb), 1), 1.0, 0.0).astype(BF16)
    assert bq & (bq - 1) == 0
    t_loc = lax.broadcasted_iota(jnp.int32, (2 * bq, kb), 0) & (bq - 1)
    causal = lax.broadcasted_iota(jnp.int32, (2 * bq, kb), 1) < t_loc
    return _lane_lo(), tri, causal


def _sb_zero_state(bq):
    return jnp.zeros((2 * bq, 1), F32), jnp.zeros((bq, V7X_LANES), F32)


def _sb_prompt_kernel(q_ref, k_ref, v_ref, o_ref):
    i = pl.program_id(2)
    bq = q_ref.shape[1]
    npair = q_ref.shape[2] // V7X_LANES
    lane_lo, tri, causal = _sb_consts(bq, bq)
    q2 = [_split_heads(q_ref[0, :, _pair(p)], lane_lo) for p in range(npair)]

    def blocks(j):
        rows = pl.ds(pl.multiple_of(j * bq, bq), bq)
        return ([k_ref[0, rows, _pair(p)] for p in range(npair)], [v_ref[0, rows, _pair(p)] for p in range(npair)])

    def sweep(j, states, mask):
        return _sb_blocks(q2, *blocks(j), tri, states, mask, lane_lo)

    def sweep2(s, states):
        j = i - 1 - 2 * s
        (ka, va), (kb, vb) = blocks(j), blocks(j - 1)
        z_a = _sb_scores(q2, ka)
        z_b = _sb_scores(q2, kb)
        c_a = _sb_cumsums(z_a, tri, None)
        c_b = _sb_cumsums(z_b, tri, None)
        states = _sb_weigh(z_a, c_a, va, states, None, lane_lo)
        return _sb_weigh(z_b, c_b, vb, states, None, lane_lo)

    states = sweep(i, (_sb_zero_state(bq),) * npair, causal)
    states = lax.fori_loop(0, i // 2, sweep2, states)
    states = lax.cond(i % 2 == 1, lambda st: sweep(0, st, None), lambda st: st, states)
    for p in range(npair):
        o_ref[0, :, _pair(p)] = states[p][1].astype(BF16)


def _sb_prompt(q, k, v):
    b, t, sw = q.shape
    bq = V7X_MXU_DIM
    width = PAIRS_PER_STEP * V7X_LANES
    qspec = pl.BlockSpec((1, bq, width), lambda bi, p, i: (bi, i, p))
    kvspec = pl.BlockSpec((1, t, width), lambda bi, p, i: (bi, 0, p))
    return pl.pallas_call(
        _sb_prompt_kernel,
        grid=(b, sw // width, t // bq),
        in_specs=[qspec, kvspec, kvspec],
        out_specs=qspec,
        out_shape=jax.ShapeDtypeStruct((b, t, sw), BF16),
        compiler_params=_params("arbitrary", "arbitrary", "arbitrary"),
        name="sb_attn_prompt",
    )(q, k, v)


def _sb_sample_kernel(q_ref, kc_ref, vc_ref, kn_ref, vn_ref, o_ref, kpad_ref, vpad_ref):
    bq = q_ref.shape[1]
    kb = kpad_ref.shape[0]
    npair = q_ref.shape[2] // V7X_LANES
    nblk = kc_ref.shape[1] // kb
    lane_lo, tri2, causal = _sb_consts(bq, kb)
    kpad_ref[...] = jnp.zeros_like(kpad_ref)
    vpad_ref[...] = jnp.zeros_like(vpad_ref)
    kpad_ref[0:bq, :] = kn_ref[0]
    vpad_ref[0:bq, :] = vn_ref[0]
    cache_rows = [slice(j * kb, (j + 1) * kb) for j in reversed(range(nblk))]
    kcache = [_load_pairs(kc_ref, r) for r in cache_rows]
    vcache = [_load_pairs(vc_ref, r) for r in cache_rows]
    for p in range(npair):
        q2 = _split_heads(q_ref[0, :, _pair(p)], lane_lo)
        kblocks = [kpad_ref[:, _pair(p)]] + [blk[p] for blk in kcache]
        vblocks = [vpad_ref[:, _pair(p)]] + [blk[p] for blk in vcache]
        z = [_dot_t(q2, kblk) for kblk in kblocks]
        sp = [_softplus(zb) for zb in z]
        sp[0] = jnp.where(causal, sp[0], 0.0)
        sp_all = jnp.concatenate(sp, axis=0)
        hi = sp_all.astype(BF16)
        lo = (sp_all - hi.astype(F32)).astype(BF16)
        c_all = _dot(jnp.concatenate([hi, lo], axis=1), tri2)
        carry = jnp.zeros((2 * bq, 1), F32)
        ws = []
        for j, zb in enumerate(z):
            c = c_all[j * 2 * bq:(j + 1) * 2 * bq]
            w = jnp.exp2((zb - c - carry) * LOG2E)
            if j == 0:
                w = jnp.where(causal, w, 0.0)
            carry = carry + c[:, 0:1]
            w = w.astype(BF16)
            ws.append(jnp.concatenate([w[:bq], w[bq:]], axis=1))
        vcat = jnp.concatenate([_split_heads(vblk, lane_lo) for vblk in vblocks], axis=0)
        o_ref[0, :, _pair(p)] = _dot(jnp.concatenate(ws, axis=1), vcat).astype(BF16)


def _sb_sample(q, kc, vc, kn, vn):
    b, t, sw = q.shape
    past = kc.shape[1]
    new = pl.BlockSpec((1, t, sw), lambda bi: (bi, 0, 0))
    cache = pl.BlockSpec((1, past) + kc.shape[2:], lambda bi: (bi, 0, 0, 0))
    return pl.pallas_call(
        _sb_sample_kernel,
        grid=(b,),
        in_specs=[new, cache, cache, new, new],
        out_specs=new,
        out_shape=jax.ShapeDtypeStruct((b, t, sw), BF16),
        scratch_shapes=[pltpu.VMEM((V7X_MXU_DIM, sw), BF16)] * 2,
        compiler_params=_params("arbitrary"),
        name="sb_attn_sample",
    )(q, kc, vc, kn, vn)


def _conv_kernel(u_ref, halo_ref, dw_ref, b_ref, g_ref, be_ref, c_ref, ext_ref, sh_ref, *, halo_is_history):
    tm = u_ref.shape[1]
    halo = halo_ref[0]
    if not halo_is_history:
        halo = jnp.where(pl.program_id(1) > 0, halo, 0.0)
    ext_ref[0:HALO_ROWS, :] = halo
    ext_ref[HALO_ROWS:, :] = u_ref[0]
    n = sh_ref.shape[1]
    for s in range(1, SUBLANES):
        sh_ref[s - 1] = ext_ref[s:s + n, :]
    first = HALO_ROWS - CONV_STATE
    rows = min(CONV_ROWS, tm)
    for r0 in range(0, tm, rows):
        y = b_ref[...]
        for w in range(CONV_WIDTH):
            s = (first + w) % SUBLANES
            a = r0 + first + w - s
            tap = ext_ref[a:a + rows, :] if s == 0 else sh_ref[s - 1, a:a + rows, :]
            y = y + tap * dw_ref[w:w + 1, :]
        mu = jnp.mean(y, axis=-1, keepdims=True)
        var = jnp.mean(jnp.square(y - mu), axis=-1, keepdims=True)
        yn = ((y - mu) * lax.rsqrt(var + LN_EPS)) * g_ref[...] + be_ref[...]
        c_ref[0, r0:r0 + rows, :] = (yn * jax.nn.sigmoid(yn)).astype(BF16)


def _conv(u, halo_src, dw, b, g, be, tm, halo_is_history):
    bsz, t, c = u.shape
    if halo_is_history:
        halo_spec = pl.BlockSpec((1, HALO_ROWS, c), lambda bi, ti: (bi, 0, 0))
    else:
        per = tm // HALO_ROWS
        halo_spec = pl.BlockSpec((1, HALO_ROWS, c), lambda bi, ti: (bi, jnp.maximum(ti * per - 1, 0), 0))
    tile = pl.BlockSpec((1, tm, c), lambda bi, ti: (bi, ti, 0))
    vec = _resident((1, c))
    return pl.pallas_call(
        functools.partial(_conv_kernel, halo_is_history=halo_is_history),
        grid=(bsz, t // tm),
        in_specs=[tile, halo_spec, _resident(dw.shape), vec, vec, vec],
        out_specs=tile,
        out_shape=jax.ShapeDtypeStruct((bsz, t, c), BF16),
        scratch_shapes=[pltpu.VMEM((HALO_ROWS + tm, c), F32),
                        pltpu.VMEM((SUBLANES - 1, HALO_ROWS + tm - SUBLANES, c), F32)],
        compiler_params=_params("arbitrary", "arbitrary"),
        name="conv_module",
    )(u, halo_src, dw, b, g, be)


def _out_mlp_kernel(x_ref, m0_ref, m1_ref, wo_ref, g_ref, wu_ref, wd_ref, gf_ref, y_ref, h_ref, *, final_norm):
    half = m0_ref.shape[-1]
    x1 = x_ref[...] + _dot(m0_ref[...], wo_ref[0:half, :]) + _dot(m1_ref[...], wo_ref[half:, :])
    h_ref[...] = _rms_rows(x1, g_ref[...]).astype(BF16)
    y_ref[...] = x1
    for c0 in range(0, wu_ref.shape[1], FF_CHUNK):
        up = _dot(h_ref[...], wu_ref[:, c0:c0 + FF_CHUNK])
        act = jnp.square(jnp.maximum(up, 0.0)).astype(BF16)
        y_ref[...] += _dot(act, wd_ref[c0:c0 + FF_CHUNK, :])
    if final_norm:
        y_ref[...] = _rms_rows(y_ref[...], gf_ref[...])


def _out_mlp(x, m0, m1, m1_block, wo, g, wu, wd, gf, tm, final_norm):
    m, d = x.shape
    half = wo.shape[0] // 2
    row = pl.BlockSpec((tm, d), lambda i: (i, 0))
    return pl.pallas_call(
        functools.partial(_out_mlp_kernel, final_norm=final_norm),
        grid=(m // tm,),
        in_specs=[row, pl.BlockSpec((tm, half), lambda i: (i, 0)), pl.BlockSpec((tm, half), lambda i: (i, m1_block)),
                  _resident(wo.shape), _resident((1, d)), _resident(wu.shape), _resident(wd.shape), _resident((1, d))],
        out_specs=row,
        out_shape=jax.ShapeDtypeStruct((m, d), F32),
        scratch_shapes=[pltpu.VMEM((tm, d), BF16)],
        compiler_params=_params("arbitrary"),
        name="out_proj_mlp",
    )(x, m0, m1, wo, g, wu, wd, gf)


def _c_in_kernel(x_ref, g_ref, w_ref, q_ref, k_ref, v_ref, kt_ref, vt_ref, h_ref, kf_ref, vf_ref, *, tail_every):
    h_ref[...] = _rms_rows(x_ref[...], g_ref[...]).astype(BF16)
    cw = q_ref.shape[-1]
    half = cw // 2
    for c0 in range(0, cw, half):
        cols = slice(c0, c0 + half)
        q_ref[:, cols] = (_dot(h_ref[...], w_ref[:, c0:c0 + half]) * SCALE).astype(BF16)
        for base, b_ref, f_ref in ((cw, k_ref, kf_ref), (2 * cw, v_ref, vf_ref)):
            f_ref[:, cols] = _dot(h_ref[...], w_ref[:, base + c0:base + c0 + half])
            b_ref[:, cols] = f_ref[:, cols].astype(BF16)

    @pl.when((pl.program_id(0) + 1) % tail_every == 0)
    def _():
        _store_heads(kt_ref, kf_ref)
        _store_heads(vt_ref, vf_ref)


def _c_in(x, g, w, t, keep, tm):
    m, d = x.shape
    cw = w.shape[1] // 3
    nh = cw // HEAD_DIM
    row = lambda n: pl.BlockSpec((tm, n), lambda i: (i, 0))
    if keep == t:
        tail_every = 1
        tail_map = lambda i: (i, 0, 0)
    else:
        assert keep == tm and t % tm == 0
        tail_every = t // tm
        tail_map = lambda i: (i // tail_every, 0, 0)
    tail = pl.BlockSpec((tm, nh, HEAD_DIM), tail_map)
    nb = m // t
    return pl.pallas_call(
        functools.partial(_c_in_kernel, tail_every=tail_every),
        grid=(m // tm,),
        in_specs=[row(d), _resident((1, d)), _resident(w.shape)],
        out_specs=[row(cw), row(cw), row(cw), tail, tail],
        out_shape=[jax.ShapeDtypeStruct((m, cw), BF16)] * 3 + [jax.ShapeDtypeStruct((nb * keep, nh, HEAD_DIM), F32)] * 2,
        scratch_shapes=[pltpu.VMEM((tm, d), BF16)] + [pltpu.VMEM((tm, cw), F32)] * 2,
        compiler_params=_params("arbitrary"),
        name="c_in_proj",
    )(x, g, w)


def _bias_kernel(rb_ref, o_ref):
    rb = rb_ref[...]
    nh = rb.shape[0]
    hi = rb.astype(BF16)
    r1 = rb - hi.astype(F32)
    mid = r1.astype(BF16)
    lo = (r1 - mid.astype(F32)).astype(BF16)
    terms = jnp.concatenate([hi, mid, lo], axis=0)
    ncol, win = rb.shape[1], o_ref.shape[2]
    s = lax.broadcasted_iota(jnp.int32, (1, win), 1)
    kk = lax.broadcasted_iota(jnp.int32, (ncol, win), 0)

    def body(r, carry):
        idx = jnp.clip(BAND_PAST + r - s, -REL_CLIP, REL_CLIP) + REL_CLIP - REL_FIRST
        kc, qc = s >> CHUNK_SHIFT, r >> CHUNK_SHIFT
        visible = (kc >= qc) & (kc <= qc + LEFT_CHUNKS)
        onehot = jnp.where((kk == idx) & visible, 1.0, 0.0).astype(BF16)
        parts = _dot(terms, onehot)
        row = (parts[0:nh] + parts[nh:2 * nh]) + parts[2 * nh:3 * nh]
        o_ref[r] = jnp.where(visible, row, -jnp.inf)
        return carry

    lax.fori_loop(0, o_ref.shape[0], body, 0, unroll=BIAS_UNROLL)


def _bias_tile(rel_bias):
    h = rel_bias.shape[0]
    cols = rel_bias[:, REL_FIRST:]
    cols = jnp.pad(cols, ((0, 0), (0, REL_COLS - cols.shape[1])))
    tile = pl.pallas_call(
        _bias_kernel,
        out_shape=jax.ShapeDtypeStruct((BAND_Q, h, BAND_WIN), F32),
        compiler_params=pltpu.CompilerParams(vmem_limit_bytes=V7X_VMEM_LIMIT),
        name="band_bias_tile",
    )(cols)
    return jnp.transpose(tile, (1, 0, 2))


def _band_core(q_pairs, kws, vws, biases, valid, lane_lo):
    bq = q_pairs[0].shape[0]
    zs = [_dot_t(_split_heads(q_pair, lane_lo), kw) for q_pair, kw in zip(q_pairs, kws)]
    es, invs = [], []
    for z, (bias0, bias1) in zip(zs, biases):
        s = z + jnp.concatenate([bias0, bias1], axis=0)
        s = jnp.where(valid, s, -jnp.inf)
        e = jnp.exp2((s - jnp.max(s, axis=-1, keepdims=True)) * LOG2E)
        invs.append(1.0 / jnp.sum(e, axis=-1, keepdims=True))
        es.append(e.astype(BF16))
    outs = []
    for e, inv, vw in zip(es, invs, vws):
        o = _dot(jnp.concatenate([e[:bq], e[bq:]], axis=1), _split_heads(vw, lane_lo))
        outs.append((o * jnp.where(lane_lo, inv[:bq], inv[bq:])).astype(BF16))
    return outs


def _band_prompt_kernel(q_ref, k_ref, v_ref, bias_ref, o_ref, kpad_ref, vpad_ref):
    i = pl.program_id(2)
    bq = q_ref.shape[1]
    npair = q_ref.shape[2] // V7X_LANES
    win = bias_ref.shape[2]
    past = win - bq

    @pl.when(i == 0)
    def _():
        kpad_ref[0:past, :] = jnp.zeros((past, kpad_ref.shape[1]), BF16)
        vpad_ref[0:past, :] = jnp.zeros((past, vpad_ref.shape[1]), BF16)
        kpad_ref[past:, :] = k_ref[0]
        vpad_ref[past:, :] = v_ref[0]

    rows = pl.ds(pl.multiple_of(i * bq, bq), win)
    valid = lax.broadcasted_iota(jnp.int32, (1, win), 1) >= past - i * bq
    lane_lo = _lane_lo()
    pairs = range(npair)
    outs = _band_core([q_ref[0, :, _pair(p)] for p in pairs], [kpad_ref[rows, _pair(p)] for p in pairs],
                      [vpad_ref[rows, _pair(p)] for p in pairs], [(bias_ref[2 * p], bias_ref[2 * p + 1]) for p in pairs],
                      valid, lane_lo)
    for p in pairs:
        o_ref[0, :, _pair(p)] = outs[p]


def _band_prompt(q, k, v, bias):
    b, t, cw = q.shape
    bq, win = bias.shape[1], bias.shape[2]
    width = PAIRS_PER_STEP * V7X_LANES
    qspec = pl.BlockSpec((1, bq, width), lambda p, bi, i: (bi, i, p))
    kvspec = pl.BlockSpec((1, t, width), lambda p, bi, i: (bi, 0, p))
    return pl.pallas_call(
        _band_prompt_kernel,
        grid=(cw // width, b, t // bq),
        in_specs=[qspec, kvspec, kvspec, pl.BlockSpec((2 * PAIRS_PER_STEP, bq, win), lambda p, bi, i: (p, 0, 0))],
        out_specs=qspec,
        out_shape=jax.ShapeDtypeStruct((b, t, cw), BF16),
        scratch_shapes=[pltpu.VMEM((win - bq + t, width), BF16)] * 2,
        compiler_params=_params("arbitrary", "arbitrary", "arbitrary"),
        name="band_attn_prompt",
    )(q, k, v, bias)


def _band_sample_kernel(q_ref, kc_ref, vc_ref, kn_ref, vn_ref, bias_ref, o_ref, kpad_ref, vpad_ref):
    t = q_ref.shape[1]
    npair = q_ref.shape[2] // V7X_LANES
    past = kc_ref.shape[1]
    win = bias_ref.shape[2]
    for pad_ref, c_ref in ((kpad_ref, kc_ref), (vpad_ref, vc_ref)):
        for p, tile in enumerate(_load_pairs(c_ref, slice(0, past))):
            pad_ref[0:past, _pair(p)] = tile
    kpad_ref[past:past + t, :] = kn_ref[0]
    vpad_ref[past:past + t, :] = vn_ref[0]
    kpad_ref[past + t:, :] = jnp.zeros((win - past - t, kpad_ref.shape[1]), BF16)
    vpad_ref[past + t:, :] = jnp.zeros((win - past - t, vpad_ref.shape[1]), BF16)
    valid = lax.broadcasted_iota(jnp.int32, (1, win), 1) < past + t
    lane_lo = _lane_lo()
    pairs = range(npair)
    outs = _band_core([q_ref[0, :, _pair(p)] for p in pairs], [kpad_ref[:, _pair(p)] for p in pairs],
                      [vpad_ref[:, _pair(p)] for p in pairs], [(bias_ref[2 * p], bias_ref[2 * p + 1]) for p in pairs],
                      valid, lane_lo)
    for p in pairs:
        o_ref[0, :, _pair(p)] = outs[p]


def _band_sample(q, kc, vc, kn, vn, bias):
    b, t, cw = q.shape
    nh = bias.shape[0]
    past, win = kc.shape[1], bias.shape[2]
    new = pl.BlockSpec((1, t, cw), lambda bi: (bi, 0, 0))
    cache = pl.BlockSpec((1, past) + kc.shape[2:], lambda bi: (bi, 0, 0, 0))
    return pl.pallas_call(
        _band_sample_kernel,
        grid=(b,),
        in_specs=[new, cache, cache, new, new, pl.BlockSpec((nh, t, win), lambda bi: (0, 0, 0))],
        out_specs=new,
        out_shape=jax.ShapeDtypeStruct((b, t, cw), BF16),
        scratch_shapes=[pltpu.VMEM((win, cw), BF16)] * 2,
        compiler_params=_params("arbitrary"),
        name="band_attn_sample",
    )(q, kc, vc, kn, vn, bias)


def kernel(x_prompt, x_sample, cache_sb_k, cache_sb_v, cache_conv, cache_band_k, cache_band_v, norm_mix, norm_ffn,
           norm_final, w_in_ab, w_out_ab, dw_w, dw_b, conv_ln_g, conv_ln_b, w_in_c, w_out_c, rel_bias, w_up, w_down):
    b, t, d = x_prompt.shape
    bs, ts, _ = x_sample.shape
    n_sb, n_c = cache_sb_k.shape[3], cache_band_k.shape[3]
    sw, cc, cw = n_sb * HEAD_DIM, dw_w.shape[2], n_c * HEAD_DIM
    past = cache_sb_k.shape[2]
    band_past = cache_band_k.shape[2]
    keep = min(BAND_PAST, t)
    assert w_in_ab.shape[0] == 1 and w_in_c.shape[0] == 1 and norm_mix.shape[0] == 2
    assert sw == cc and sw + cc == d and cw == d and w_in_ab.shape[2] == 3 * sw + 2 * cc
    assert t % ROW_TILE == 0 and ts % 16 == 0 and ts <= V7X_MXU_DIM
    assert past % V7X_MXU_DIM == 0 and band_past == BAND_PAST and band_past + ts <= BAND_WIN

    tm_p, tm_s = ROW_TILE, bs * ts
    vec = lambda a: a.reshape(1, -1)
    w_in0, w_out0 = w_in_ab[0].astype(BF16), w_out_ab[0].astype(BF16)
    w_in1, w_out1 = w_in_c[0].astype(BF16), w_out_c[0].astype(BF16)
    w_up_b, w_down_b = w_up.astype(BF16), w_down.astype(BF16)
    dw_pad = jnp.pad(dw_w[0], ((0, HALO_ROWS - CONV_WIDTH), (0, 0)))
    conv_vecs = (vec(dw_b[0]), vec(conv_ln_g[0]), vec(conv_ln_b[0]))
    bias = _bias_tile(rel_bias[0])

    def layer0(x, tm, attn, conv):
        q, kb, vb, k, v, u = _ab_in(x, vec(norm_mix[0]), w_in0, sw, cc, tm)
        a = attn(q, kb, vb)
        c = conv(u)
        x2 = _out_mlp(x, a, c, 0, w_out0, vec(norm_ffn[0]), w_up_b[0], w_down_b[0], vec(norm_final), tm, False)
        return x2, k, v, u

    def layer1(x, tm, tlen, tail, attn):
        q, k, v, kt, vt = _c_in(x, vec(norm_mix[1]), w_in1, tlen, tail, tm)
        o = attn(q, k, v)
        y = _out_mlp(x, o, o, 1, w_out1, vec(norm_ffn[1]), w_up_b[1], w_down_b[1], vec(norm_final), tm, True)
        return y, kt, vt

    shp = lambda z, n: z.reshape(b, t, n)
    xp2, kp, vp, up = layer0(
        x_prompt.reshape(b * t, d), tm_p,
        lambda q, k, v: _sb_prompt(shp(q, sw), shp(k, sw), shp(v, sw)).reshape(b * t, sw),
        lambda u: _conv(shp(u, cc), shp(u, cc), dw_pad, *conv_vecs, CONV_TILE, False).reshape(b * t, cc))
    yp, bkp, bvp = layer1(
        xp2, tm_p, t, keep,
        lambda q, k, v: _band_prompt(shp(q, cw), shp(k, cw), shp(v, cw), bias).reshape(b * t, cw))

    shs = lambda z, n: z.reshape(bs, ts, n)
    conv_hist = jnp.pad(cache_conv[0], ((0, 0), (HALO_ROWS - CONV_STATE, 0), (0, 0)))
    ck, cv = cache_sb_k[0], cache_sb_v[0]
    xs2, ks, vs, us = layer0(
        x_sample.reshape(bs * ts, d), tm_s,
        lambda q, k, v: _sb_sample(shs(q, sw), ck, cv, shs(k, sw), shs(v, sw)).reshape(bs * ts, sw),
        lambda u: _conv(shs(u, cc), conv_hist, dw_pad, *conv_vecs, ts, True).reshape(bs * ts, cc))
    bck, bcv = cache_band_k[0], cache_band_v[0]
    ys, bks, bvs = layer1(
        xs2, tm_s, ts, ts,
        lambda q, k, v: _band_sample(shs(q, cw), bck, bcv, shs(k, cw), shs(v, cw), bias).reshape(bs * ts, cw))

    new_conv_p = up.reshape(b, t, cc)[:, t - CONV_STATE:]
    new_conv_s = jnp.concatenate([cache_conv[0], us.reshape(bs, ts, cc)], axis=1)[:, ts:]
    return (yp.reshape(b, t, d), ys.reshape(bs, ts, d),
            kp.reshape(1, b, t, n_sb, HEAD_DIM), vp.reshape(1, b, t, n_sb, HEAD_DIM), new_conv_p[None],
            bkp.reshape(1, b, keep, n_c, HEAD_DIM), bvp.reshape(1, b, keep, n_c, HEAD_DIM),
            ks.reshape(1, bs, ts, n_sb, HEAD_DIM), vs.reshape(1, bs, ts, n_sb, HEAD_DIM), new_conv_s[None],
            bks.reshape(1, bs, ts, n_c, HEAD_DIM), bvs.reshape(1, bs, ts, n_c, HEAD_DIM))
```

```python
import functools

import jax
import jax.numpy as jnp
from jax import lax
from jax.experimental import pallas as pl
from jax.experimental.pallas import tpu as pltpu

F32 = jnp.float32
BF16 = jnp.bfloat16

HEAD_DIM = 64
CHUNK = 64
CHUNK_SHIFT = 6
LEFT_CHUNKS = 8
BAND_PAST = LEFT_CHUNKS * CHUNK
REL_CLIP = 128
CONV_WIDTH = 31
CONV_STATE = CONV_WIDTH - 1
RMS_EPS = 1e-6
LN_EPS = 1e-5
LOG2E = 1.4426950408889634
SCALE = HEAD_DIM ** -0.5

V7X_LANES = 128
SUBLANES = 8
V7X_MXU_DIM = 256
V7X_VMEM_LIMIT = 56 * 1024 * 1024
ROW_TILE = 512
FF_CHUNK = 512
CONV_TILE = 256
HALO_ROWS = 32
CONV_ROWS = 64
PAIRS_PER_STEP = 4
BAND_Q = 256
BAND_WIN = BAND_PAST + BAND_Q
REL_FIRST = REL_CLIP + 1 - CHUNK
REL_COLS = 256
BIAS_UNROLL = 4


def _params(*semantics):
    return pltpu.CompilerParams(dimension_semantics=semantics, vmem_limit_bytes=V7X_VMEM_LIMIT)


def _dot(a, b):
    return jnp.dot(a, b, preferred_element_type=F32)


def _dot_t(a, b):
    return lax.dot_general(a, b, (((1,), (1,)), ((), ())), preferred_element_type=F32)


def _rms_rows(x, g):
    r = lax.rsqrt(jnp.mean(x * x, axis=-1, keepdims=True) + RMS_EPS)
    return (x * r) * g


def _resident(shape):
    return pl.BlockSpec(shape, lambda *_: (0,) * len(shape), pipeline_mode=pl.Buffered(1))


def _lane_lo():
    return lax.broadcasted_iota(jnp.int32, (1, V7X_LANES), 1) < HEAD_DIM


def _split_heads(xp, lane_lo):
    zero = jnp.zeros_like(xp)
    return jnp.concatenate([jnp.where(lane_lo, xp, zero), jnp.where(lane_lo, zero, xp)], axis=0)


def _pair(p):
    return slice(p * V7X_LANES, (p + 1) * V7X_LANES)


def _store_heads(dst_ref, src_ref):
    nh = dst_ref.shape[1]
    per_head = jnp.stack([src_ref[:, h * HEAD_DIM:(h + 1) * HEAD_DIM] for h in range(nh)], axis=0)
    dst_ref[...] = jnp.swapaxes(per_head, 0, 1)


def _ab_in_kernel(x_ref, g_ref, w_ref, q_ref, kb_ref, vb_ref, k_ref, v_ref, u_ref, h_ref, kf_ref, vf_ref):
    h_ref[...] = _rms_rows(x_ref[...], g_ref[...]).astype(BF16)
    sw = q_ref.shape[-1]
    cw = u_ref.shape[-1]
    proj = lambda c0, n: _dot(h_ref[...], w_ref[:, c0:c0 + n])
    q_ref[...] = (proj(0, sw) * SCALE).astype(BF16)
    for c0, b_ref, o_ref, f_ref in ((sw, kb_ref, k_ref, kf_ref), (2 * sw, vb_ref, v_ref, vf_ref)):
        f_ref[...] = proj(c0, sw)
        b_ref[...] = f_ref[...].astype(BF16)
        _store_heads(o_ref, f_ref)
    u_ref[...] = proj(3 * sw, cw) * jax.nn.sigmoid(proj(3 * sw + cw, cw))


def _ab_in(x, g, w, sw, cw, tm):
    m, d = x.shape
    nh = sw // HEAD_DIM
    row = lambda n: pl.BlockSpec((tm, n), lambda i: (i, 0))
    heads = pl.BlockSpec((tm, nh, HEAD_DIM), lambda i: (i, 0, 0))
    return pl.pallas_call(
        _ab_in_kernel,
        grid=(m // tm,),
        in_specs=[row(d), _resident((1, d)), _resident(w.shape)],
        out_specs=[row(sw)] * 3 + [heads] * 2 + [row(cw)],
        out_shape=[jax.ShapeDtypeStruct((m, sw), BF16)] * 3 + [jax.ShapeDtypeStruct((m, nh, HEAD_DIM), F32)] * 2
        + [jax.ShapeDtypeStruct((m, cw), F32)],
        scratch_shapes=[pltpu.VMEM((tm, d), BF16)] + [pltpu.VMEM((tm, sw), F32)] * 2,
        compiler_params=_params("arbitrary"),
        name="ab_in_proj",
    )(x, g, w)


def _softplus(z):
    return jnp.maximum(z, 0.0) + jnp.log(1.0 + jnp.exp2(jnp.abs(z) * -LOG2E))


def _sb_scores(q2s, kblks):
    return [_dot_t(q2, kblk) for q2, kblk in zip(q2s, kblks)]


def _sb_cumsums(zs, tri2, mask):
    cs = []
    for z in zs:
        sp = _softplus(z)
        if mask is not None:
            sp = jnp.where(mask, sp, 0.0)
        hi = sp.astype(BF16)
        lo = (sp - hi.astype(F32)).astype(BF16)
        cs.append(_dot(jnp.concatenate([hi, lo], axis=1), tri2))
    return cs


def _sb_weigh(zs, cs, vblks, states, mask, lane_lo):
    out = []
    for z, c, vblk, (carry, acc) in zip(zs, cs, vblks, states):
        bq = acc.shape[0]
        w = jnp.exp2((z - c - carry) * LOG2E)
        if mask is not None:
            w = jnp.where(mask, w, 0.0)
        w = w.astype(BF16)
        wcat = jnp.concatenate([w[:bq], w[bq:]], axis=1)
        out.append((carry + c[:, 0:1], acc + _dot(wcat, _split_heads(vblk, lane_lo))))
    return tuple(out)


def _sb_blocks(q2s, kblks, vblks, tri2, states, mask, lane_lo):
    zs = _sb_scores(q2s, kblks)
    return _sb_weigh(zs, _sb_cumsums(zs, tri2, mask), vblks, states, mask, lane_lo)


def _sb_consts(bq, kb):
    tri = jnp.where((lax.broadcasted_iota(jnp.int32, (2 * kb, kb), 0) & (kb - 1))
                    >= lax.broadcasted_iota(jnp.int32, (2 * kb, kb), 1), 1.0, 0.0).astype(BF16)
    assert bq & (bq - 1) == 0
    t_loc = lax.broadcasted_iota(jnp.int32, (2 * bq, kb), 0) & (bq - 1)
    causal = lax.broadcasted_iota(jnp.int32, (2 * bq, kb), 1) < t_loc
    return _lane_lo(), tri, causal


def _sb_zero_state(bq):
    return jnp.zeros((2 * bq, 1), F32), jnp.zeros((bq, V7X_LANES), F32)


def _sb_prompt_kernel(q_ref, k_ref, v_ref, o_ref):
    i = pl.program_id(2)
    bq = q_ref.shape[1]
    npair = q_ref.shape[2] // V7X_LANES
    lane_lo, tri, causal = _sb_consts(bq, bq)
    q2 = [_split_heads(q_ref[0, :, _pair(p)], lane_lo) for p in range(npair)]

    def blocks(j):
        rows = pl.ds(pl.multiple_of(j * bq, bq), bq)
        return ([k_ref[0, rows, _pair(p)] for p in range(npair)], [v_ref[0, rows, _pair(p)] for p in range(npair)])

    def sweep(j, states, mask):
        return _sb_blocks(q2, *blocks(j), tri, states, mask, lane_lo)

    def sweep2(s, states):
        j = i - 1 - 2 * s
        (ka, va), (kb, vb) = blocks(j), blocks(j - 1)
        z_a = _sb_scores(q2, ka)
        z_b = _sb_scores(q2, kb)
        c_a = _sb_cumsums(z_a, tri, None)
        c_b = _sb_cumsums(z_b, tri, None)
        states = _sb_weigh(z_a, c_a, va, states, None, lane_lo)
        return _sb_weigh(z_b, c_b, vb, states, None, lane_lo)

    states = sweep(i, (_sb_zero_state(bq),) * npair, causal)
    states = lax.fori_loop(0, i // 2, sweep2, states)
    states = lax.cond(i % 2 == 1, lambda st: sweep(0, st, None), lambda st: st, states)
    for p in range(npair):
        o_ref[0, :, _pair(p)] = states[p][1].astype(BF16)


def _sb_prompt(q, k, v):
    b, t, sw = q.shape
    bq = V7X_MXU_DIM
    width = PAIRS_PER_STEP * V7X_LANES
    qspec = pl.BlockSpec((1, bq, width), lambda bi, p, i: (bi, i, p))
    kvspec = pl.BlockSpec((1, t, width), lambda bi, p, i: (bi, 0, p))
    return pl.pallas_call(
        _sb_prompt_kernel,
        grid=(b, sw // width, t // bq),
        in_specs=[qspec, kvspec, kvspec],
        out_specs=qspec,
        out_shape=jax.ShapeDtypeStruct((b, t, sw), BF16),
        compiler_params=_params("arbitrary", "arbitrary", "arbitrary"),
        name="sb_attn_prompt",
    )(q, k, v)


def _pair_rows(ref, p):
    return jnp.concatenate([ref[0, 2 * p], ref[0, 2 * p + 1]], axis=0).astype(BF16)


def _split_heads_t(xt, row_lo):
    zero = jnp.zeros_like(xt)
    return jnp.concatenate([jnp.where(row_lo, xt, zero), jnp.where(row_lo, zero, xt)], axis=1)


def _row_lo():
    return lax.broadcasted_iota(jnp.int32, (V7X_LANES, 1), 0) < HEAD_DIM


def _sb_sample_kernel(q_ref, kc_ref, vc_ref, kn_ref, vn_ref, o_ref, kpad_ref, vpad_ref):
    bq = q_ref.shape[1]
    kb = kpad_ref.shape[0]
    npair = q_ref.shape[2] // V7X_LANES
    nblk = kc_ref.shape[3] // kb
    lane_lo, tri2, causal = _sb_consts(bq, kb)
    row_lo = _row_lo()
    kpad_ref[...] = jnp.zeros_like(kpad_ref)
    vpad_ref[...] = jnp.zeros_like(vpad_ref)
    kpad_ref[0:bq, :] = kn_ref[0]
    vpad_ref[0:bq, :] = vn_ref[0]
    cols = [slice(j * kb, (j + 1) * kb) for j in reversed(range(nblk))]
    for p in range(npair):
        q2 = _split_heads(q_ref[0, :, _pair(p)], lane_lo)
        kt, vt = _pair_rows(kc_ref, p), _pair_rows(vc_ref, p)
        z = [_dot_t(q2, kpad_ref[:, _pair(p)])] + [_dot(q2, kt[:, c]) for c in cols]
        sp = [_softplus(zb) for zb in z]
        sp[0] = jnp.where(causal, sp[0], 0.0)
        sp_all = jnp.concatenate(sp, axis=0)
        hi = sp_all.astype(BF16)
        lo = (sp_all - hi.astype(F32)).astype(BF16)
        c_all = _dot(jnp.concatenate([hi, lo], axis=1), tri2)
        carry = jnp.zeros((2 * bq, 1), F32)
        ws = []
        for j, zb in enumerate(z):
            c = c_all[j * 2 * bq:(j + 1) * 2 * bq]
            w = jnp.exp2((zb - c - carry) * LOG2E)
            if j == 0:
                w = jnp.where(causal, w, 0.0)
            carry = carry + c[:, 0:1]
            w = w.astype(BF16)
            ws.append(jnp.concatenate([w[:bq], w[bq:]], axis=1))
        vt_cat = jnp.concatenate([_split_heads_t(vt[:, c], row_lo) for c in cols], axis=1)
        acc = _dot(ws[0], _split_heads(vpad_ref[:, _pair(p)], lane_lo)) + _dot_t(jnp.concatenate(ws[1:], axis=1), vt_cat)
        o_ref[0, :, _pair(p)] = acc.astype(BF16)


def _sb_sample(q, kc, vc, kn, vn):
    b, t, sw = q.shape
    new = pl.BlockSpec((1, t, sw), lambda bi: (bi, 0, 0))
    cache = pl.BlockSpec((1,) + kc.shape[1:], lambda bi: (bi, 0, 0, 0))
    return pl.pallas_call(
        _sb_sample_kernel,
        grid=(b,),
        in_specs=[new, cache, cache, new, new],
        out_specs=new,
        out_shape=jax.ShapeDtypeStruct((b, t, sw), BF16),
        scratch_shapes=[pltpu.VMEM((V7X_MXU_DIM, sw), BF16)] * 2,
        compiler_params=_params("arbitrary"),
        name="sb_attn_sample",
    )(q, kc, vc, kn, vn)


def _conv_kernel(u_ref, halo_ref, dw_ref, b_ref, g_ref, be_ref, c_ref, ext_ref, sh_ref, *, halo_is_history):
    tm = u_ref.shape[1]
    halo = halo_ref[0]
    if not halo_is_history:
        halo = jnp.where(pl.program_id(1) > 0, halo, 0.0)
    ext_ref[0:HALO_ROWS, :] = halo
    ext_ref[HALO_ROWS:, :] = u_ref[0]
    n = sh_ref.shape[1]
    for s in range(1, SUBLANES):
        sh_ref[s - 1] = ext_ref[s:s + n, :]
    first = HALO_ROWS - CONV_STATE
    rows = min(CONV_ROWS, tm)
    for r0 in range(0, tm, rows):
        y = b_ref[...]
        for w in range(CONV_WIDTH):
            s = (first + w) % SUBLANES
            a = r0 + first + w - s
            tap = ext_ref[a:a + rows, :] if s == 0 else sh_ref[s - 1, a:a + rows, :]
            y = y + tap * dw_ref[w:w + 1, :]
        mu = jnp.mean(y, axis=-1, keepdims=True)
        var = jnp.mean(jnp.square(y - mu), axis=-1, keepdims=True)
        yn = ((y - mu) * lax.rsqrt(var + LN_EPS)) * g_ref[...] + be_ref[...]
        c_ref[0, r0:r0 + rows, :] = (yn * jax.nn.sigmoid(yn)).astype(BF16)


def _conv(u, halo_src, dw, b, g, be, tm, halo_is_history):
    bsz, t, c = u.shape
    if halo_is_history:
        halo_spec = pl.BlockSpec((1, HALO_ROWS, c), lambda bi, ti: (bi, 0, 0))
    else:
        per = tm // HALO_ROWS
        halo_spec = pl.BlockSpec((1, HALO_ROWS, c), lambda bi, ti: (bi, jnp.maximum(ti * per - 1, 0), 0))
    tile = pl.BlockSpec((1, tm, c), lambda bi, ti: (bi, ti, 0))
    vec = _resident((1, c))
    return pl.pallas_call(
        functools.partial(_conv_kernel, halo_is_history=halo_is_history),
        grid=(bsz, t // tm),
        in_specs=[tile, halo_spec, _resident(dw.shape), vec, vec, vec],
        out_specs=tile,
        out_shape=jax.ShapeDtypeStruct((bsz, t, c), BF16),
        scratch_shapes=[pltpu.VMEM((HALO_ROWS + tm, c), F32),
                        pltpu.VMEM((SUBLANES - 1, HALO_ROWS + tm - SUBLANES, c), F32)],
        compiler_params=_params("arbitrary", "arbitrary"),
        name="conv_module",
    )(u, halo_src, dw, b, g, be)


def _out_mlp_kernel(x_ref, m0_ref, m1_ref, wo_ref, g_ref, wu_ref, wd_ref, gf_ref, y_ref, h_ref, *, final_norm):
    half = m0_ref.shape[-1]
    x1 = x_ref[...] + _dot(m0_ref[...], wo_ref[0:half, :]) + _dot(m1_ref[...], wo_ref[half:, :])
    h_ref[...] = _rms_rows(x1, g_ref[...]).astype(BF16)
    y_ref[...] = x1
    for c0 in range(0, wu_ref.shape[1], FF_CHUNK):
        up = _dot(h_ref[...], wu_ref[:, c0:c0 + FF_CHUNK])
        act = jnp.square(jnp.maximum(up, 0.0)).astype(BF16)
        y_ref[...] += _dot(act, wd_ref[c0:c0 + FF_CHUNK, :])
    if final_norm:
        y_ref[...] = _rms_rows(y_ref[...], gf_ref[...])


def _out_mlp(x, m0, m1, m1_block, wo, g, wu, wd, gf, tm, final_norm):
    m, d = x.shape
    half = wo.shape[0] // 2
    row = pl.BlockSpec((tm, d), lambda i: (i, 0))
    return pl.pallas_call(
        functools.partial(_out_mlp_kernel, final_norm=final_norm),
        grid=(m // tm,),
        in_specs=[row, pl.BlockSpec((tm, half), lambda i: (i, 0)), pl.BlockSpec((tm, half), lambda i: (i, m1_block)),
                  _resident(wo.shape), _resident((1, d)), _resident(wu.shape), _resident(wd.shape), _resident((1, d))],
        out_specs=row,
        out_shape=jax.ShapeDtypeStruct((m, d), F32),
        scratch_shapes=[pltpu.VMEM((tm, d), BF16)],
        compiler_params=_params("arbitrary"),
        name="out_proj_mlp",
    )(x, m0, m1, wo, g, wu, wd, gf)


def _c_in_kernel(x_ref, g_ref, w_ref, q_ref, k_ref, v_ref, kt_ref, vt_ref, h_ref, kf_ref, vf_ref, *, tail_every):
    h_ref[...] = _rms_rows(x_ref[...], g_ref[...]).astype(BF16)
    cw = q_ref.shape[-1]
    half = cw // 2
    for c0 in range(0, cw, half):
        cols = slice(c0, c0 + half)
        q_ref[:, cols] = (_dot(h_ref[...], w_ref[:, c0:c0 + half]) * SCALE).astype(BF16)
        for base, b_ref, f_ref in ((cw, k_ref, kf_ref), (2 * cw, v_ref, vf_ref)):
            f_ref[:, cols] = _dot(h_ref[...], w_ref[:, base + c0:base + c0 + half])
            b_ref[:, cols] = f_ref[:, cols].astype(BF16)

    @pl.when((pl.program_id(0) + 1) % tail_every == 0)
    def _():
        _store_heads(kt_ref, kf_ref)
        _store_heads(vt_ref, vf_ref)


def _c_in(x, g, w, t, keep, tm):
    m, d = x.shape
    cw = w.shape[1] // 3
    nh = cw // HEAD_DIM
    row = lambda n: pl.BlockSpec((tm, n), lambda i: (i, 0))
    if keep == t:
        tail_every = 1
        tail_map = lambda i: (i, 0, 0)
    else:
        assert keep == tm and t % tm == 0
        tail_every = t // tm
        tail_map = lambda i: (i // tail_every, 0, 0)
    tail = pl.BlockSpec((tm, nh, HEAD_DIM), tail_map)
    nb = m // t
    return pl.pallas_call(
        functools.partial(_c_in_kernel, tail_every=tail_every),
        grid=(m // tm,),
        in_specs=[row(d), _resident((1, d)), _resident(w.shape)],
        out_specs=[row(cw), row(cw), row(cw), tail, tail],
        out_shape=[jax.ShapeDtypeStruct((m, cw), BF16)] * 3 + [jax.ShapeDtypeStruct((nb * keep, nh, HEAD_DIM), F32)] * 2,
        scratch_shapes=[pltpu.VMEM((tm, d), BF16)] + [pltpu.VMEM((tm, cw), F32)] * 2,
        compiler_params=_params("arbitrary"),
        name="c_in_proj",
    )(x, g, w)


def _bias_kernel(rb_ref, o_ref):
    rb = rb_ref[...]
    nh = rb.shape[0]
    hi = rb.astype(BF16)
    r1 = rb - hi.astype(F32)
    mid = r1.astype(BF16)
    lo = (r1 - mid.astype(F32)).astype(BF16)
    terms = jnp.concatenate([hi, mid, lo], axis=0)
    ncol, win = rb.shape[1], o_ref.shape[2]
    s = lax.broadcasted_iota(jnp.int32, (1, win), 1)
    kk = lax.broadcasted_iota(jnp.int32, (ncol, win), 0)

    def body(r, carry):
        idx = jnp.clip(BAND_PAST + r - s, -REL_CLIP, REL_CLIP) + REL_CLIP - REL_FIRST
        kc, qc = s >> CHUNK_SHIFT, r >> CHUNK_SHIFT
        visible = (kc >= qc) & (kc <= qc + LEFT_CHUNKS)
        onehot = jnp.where((kk == idx) & visible, 1.0, 0.0).astype(BF16)
        parts = _dot(terms, onehot)
        row = (parts[0:nh] + parts[nh:2 * nh]) + parts[2 * nh:3 * nh]
        o_ref[r] = jnp.where(visible, row, -jnp.inf)
        return carry

    lax.fori_loop(0, o_ref.shape[0], body, 0, unroll=BIAS_UNROLL)


def _bias_tile(rel_bias):
    h = rel_bias.shape[0]
    cols = rel_bias[:, REL_FIRST:]
    cols = jnp.pad(cols, ((0, 0), (0, REL_COLS - cols.shape[1])))
    tile = pl.pallas_call(
        _bias_kernel,
        out_shape=jax.ShapeDtypeStruct((BAND_Q, h, BAND_WIN), F32),
        compiler_params=pltpu.CompilerParams(vmem_limit_bytes=V7X_VMEM_LIMIT),
        name="band_bias_tile",
    )(cols)
    return jnp.transpose(tile, (1, 0, 2))


def _band_core(q_pairs, kws, vws, biases, valid, lane_lo):
    bq = q_pairs[0].shape[0]
    zs = [_dot_t(_split_heads(q_pair, lane_lo), kw) for q_pair, kw in zip(q_pairs, kws)]
    es, invs = [], []
    for z, (bias0, bias1) in zip(zs, biases):
        s = z + jnp.concatenate([bias0, bias1], axis=0)
        s = jnp.where(valid, s, -jnp.inf)
        e = jnp.exp2((s - jnp.max(s, axis=-1, keepdims=True)) * LOG2E)
        invs.append(1.0 / jnp.sum(e, axis=-1, keepdims=True))
        es.append(e.astype(BF16))
    outs = []
    for e, inv, vw in zip(es, invs, vws):
        o = _dot(jnp.concatenate([e[:bq], e[bq:]], axis=1), _split_heads(vw, lane_lo))
        outs.append((o * jnp.where(lane_lo, inv[:bq], inv[bq:])).astype(BF16))
    return outs


def _band_prompt_kernel(q_ref, k_ref, v_ref, bias_ref, o_ref, kpad_ref, vpad_ref):
    i = pl.program_id(2)
    bq = q_ref.shape[1]
    npair = q_ref.shape[2] // V7X_LANES
    win = bias_ref.shape[2]
    past = win - bq

    @pl.when(i == 0)
    def _():
        kpad_ref[0:past, :] = jnp.zeros((past, kpad_ref.shape[1]), BF16)
        vpad_ref[0:past, :] = jnp.zeros((past, vpad_ref.shape[1]), BF16)
        kpad_ref[past:, :] = k_ref[0]
        vpad_ref[past:, :] = v_ref[0]

    rows = pl.ds(pl.multiple_of(i * bq, bq), win)
    valid = lax.broadcasted_iota(jnp.int32, (1, win), 1) >= past - i * bq
    lane_lo = _lane_lo()
    pairs = range(npair)
    outs = _band_core([q_ref[0, :, _pair(p)] for p in pairs], [kpad_ref[rows, _pair(p)] for p in pairs],
                      [vpad_ref[rows, _pair(p)] for p in pairs], [(bias_ref[2 * p], bias_ref[2 * p + 1]) for p in pairs],
                      valid, lane_lo)
    for p in pairs:
        o_ref[0, :, _pair(p)] = outs[p]


def _band_prompt(q, k, v, bias):
    b, t, cw = q.shape
    bq, win = bias.shape[1], bias.shape[2]
    width = PAIRS_PER_STEP * V7X_LANES
    qspec = pl.BlockSpec((1, bq, width), lambda p, bi, i: (bi, i, p))
    kvspec = pl.BlockSpec((1, t, width), lambda p, bi, i: (bi, 0, p))
    return pl.pallas_call(
        _band_prompt_kernel,
        grid=(cw // width, b, t // bq),
        in_specs=[qspec, kvspec, kvspec, pl.BlockSpec((2 * PAIRS_PER_STEP, bq, win), lambda p, bi, i: (p, 0, 0))],
        out_specs=qspec,
        out_shape=jax.ShapeDtypeStruct((b, t, cw), BF16),
        scratch_shapes=[pltpu.VMEM((win - bq + t, width), BF16)] * 2,
        compiler_params=_params("arbitrary", "arbitrary", "arbitrary"),
        name="band_attn_prompt",
    )(q, k, v, bias)


def _band_sample_kernel(q_ref, kc_ref, vc_ref, kn_ref, vn_ref, bias_ref, o_ref, kpad_ref, vpad_ref):
    t = q_ref.shape[1]
    npair = q_ref.shape[2] // V7X_LANES
    past = kc_ref.shape[3]
    win = bias_ref.shape[2]
    kpad_ref[...] = jnp.zeros_like(kpad_ref)
    vpad_ref[...] = jnp.zeros_like(vpad_ref)
    kpad_ref[0:t, :] = kn_ref[0]
    vpad_ref[0:t, :] = vn_ref[0]
    valid = lax.broadcasted_iota(jnp.int32, (1, win), 1) < past + t
    lane_lo, row_lo = _lane_lo(), _row_lo()
    pairs = range(npair)
    q2s = [_split_heads(q_ref[0, :, _pair(p)], lane_lo) for p in pairs]
    zs = [jnp.concatenate([_dot(q2s[p], _pair_rows(kc_ref, p)), _dot_t(q2s[p], kpad_ref[:, _pair(p)])], axis=1)
          for p in pairs]
    for p in pairs:
        s = zs[p] + jnp.concatenate([bias_ref[2 * p], bias_ref[2 * p + 1]], axis=0)
        s = jnp.where(valid, s, -jnp.inf)
        e = jnp.exp2((s - jnp.max(s, axis=-1, keepdims=True)) * LOG2E)
        inv = 1.0 / jnp.sum(e, axis=-1, keepdims=True)
        e = e.astype(BF16)
        e_cache = jnp.concatenate([e[:t, :past], e[t:, :past]], axis=1)
        e_new = jnp.concatenate([e[:t, past:], e[t:, past:]], axis=1)
        o = (_dot_t(e_cache, _split_heads_t(_pair_rows(vc_ref, p), row_lo))
             + _dot(e_new, _split_heads(vpad_ref[:, _pair(p)], lane_lo)))
        o_ref[0, :, _pair(p)] = (o * jnp.where(lane_lo, inv[:t], inv[t:])).astype(BF16)


def _band_sample(q, kc, vc, kn, vn, bias):
    b, t, cw = q.shape
    nh = bias.shape[0]
    past, win = kc.shape[3], bias.shape[2]
    new = pl.BlockSpec((1, t, cw), lambda bi: (bi, 0, 0))
    cache = pl.BlockSpec((1,) + kc.shape[1:], lambda bi: (bi, 0, 0, 0))
    return pl.pallas_call(
        _band_sample_kernel,
        grid=(b,),
        in_specs=[new, cache, cache, new, new, pl.BlockSpec((nh, t, win), lambda bi: (0, 0, 0))],
        out_specs=new,
        out_shape=jax.ShapeDtypeStruct((b, t, cw), BF16),
        scratch_shapes=[pltpu.VMEM((win - past, cw), BF16)] * 2,
        compiler_params=_params("arbitrary"),
        name="band_attn_sample",
    )(q, kc, vc, kn, vn, bias)


def kernel(x_prompt, x_sample, cache_sb_k, cache_sb_v, cache_conv, cache_band_k, cache_band_v, norm_mix, norm_ffn,
           norm_final, w_in_ab, w_out_ab, dw_w, dw_b, conv_ln_g, conv_ln_b, w_in_c, w_out_c, rel_bias, w_up, w_down):
    b, t, d = x_prompt.shape
    bs, ts, _ = x_sample.shape
    n_sb, n_c = cache_sb_k.shape[3], cache_band_k.shape[3]
    sw, cc, cw = n_sb * HEAD_DIM, dw_w.shape[2], n_c * HEAD_DIM
    past = cache_sb_k.shape[2]
    band_past = cache_band_k.shape[2]
    keep = min(BAND_PAST, t)
    assert w_in_ab.shape[0] == 1 and w_in_c.shape[0] == 1 and norm_mix.shape[0] == 2
    assert sw == cc and sw + cc == d and cw == d and w_in_ab.shape[2] == 3 * sw + 2 * cc
    assert t % ROW_TILE == 0 and ts % 16 == 0 and ts <= V7X_MXU_DIM
    assert past % V7X_MXU_DIM == 0 and band_past == BAND_PAST and band_past + ts <= BAND_WIN

    tm_p, tm_s = ROW_TILE, bs * ts
    vec = lambda a: a.reshape(1, -1)
    w_in0, w_out0 = w_in_ab[0].astype(BF16), w_out_ab[0].astype(BF16)
    w_in1, w_out1 = w_in_c[0].astype(BF16), w_out_c[0].astype(BF16)
    w_up_b, w_down_b = w_up.astype(BF16), w_down.astype(BF16)
    dw_pad = jnp.pad(dw_w[0], ((0, HALO_ROWS - CONV_WIDTH), (0, 0)))
    conv_vecs = (vec(dw_b[0]), vec(conv_ln_g[0]), vec(conv_ln_b[0]))
    bias = _bias_tile(rel_bias[0])

    def layer0(x, tm, attn, conv):
        q, kb, vb, k, v, u = _ab_in(x, vec(norm_mix[0]), w_in0, sw, cc, tm)
        a = attn(q, kb, vb)
        c = conv(u)
        x2 = _out_mlp(x, a, c, 0, w_out0, vec(norm_ffn[0]), w_up_b[0], w_down_b[0], vec(norm_final), tm, False)
        return x2, k, v, u

    def layer1(x, tm, tlen, tail, attn):
        q, k, v, kt, vt = _c_in(x, vec(norm_mix[1]), w_in1, tlen, tail, tm)
        o = attn(q, k, v)
        y = _out_mlp(x, o, o, 1, w_out1, vec(norm_ffn[1]), w_up_b[1], w_down_b[1], vec(norm_final), tm, True)
        return y, kt, vt

    shp = lambda z, n: z.reshape(b, t, n)
    xp2, kp, vp, up = layer0(
        x_prompt.reshape(b * t, d), tm_p,
        lambda q, k, v: _sb_prompt(shp(q, sw), shp(k, sw), shp(v, sw)).reshape(b * t, sw),
        lambda u: _conv(shp(u, cc), shp(u, cc), dw_pad, *conv_vecs, CONV_TILE, False).reshape(b * t, cc))
    yp, bkp, bvp = layer1(
        xp2, tm_p, t, keep,
        lambda q, k, v: _band_prompt(shp(q, cw), shp(k, cw), shp(v, cw), bias).reshape(b * t, cw))

    shs = lambda z, n: z.reshape(bs, ts, n)
    conv_hist = jnp.pad(cache_conv[0], ((0, 0), (HALO_ROWS - CONV_STATE, 0), (0, 0)))
    by_head = lambda c: jnp.transpose(c[0], (0, 2, 3, 1))
    ck, cv = by_head(cache_sb_k), by_head(cache_sb_v)
    xs2, ks, vs, us = layer0(
        x_sample.reshape(bs * ts, d), tm_s,
        lambda q, k, v: _sb_sample(shs(q, sw), ck, cv, shs(k, sw), shs(v, sw)).reshape(bs * ts, sw),
        lambda u: _conv(shs(u, cc), conv_hist, dw_pad, *conv_vecs, ts, True).reshape(bs * ts, cc))
    bck, bcv = by_head(cache_band_k), by_head(cache_band_v)
    ys, bks, bvs = layer1(
        xs2, tm_s, ts, ts,
        lambda q, k, v: _band_sample(shs(q, cw), bck, bcv, shs(k, cw), shs(v, cw), bias).reshape(bs * ts, cw))

    new_conv_p = up.reshape(b, t, cc)[:, t - CONV_STATE:]
    new_conv_s = jnp.concatenate([cache_conv[0], us.reshape(bs, ts, cc)], axis=1)[:, ts:]
    return (yp.reshape(b, t, d), ys.reshape(bs, ts, d),
            kp.reshape(1, b, t, n_sb, HEAD_DIM), vp.reshape(1, b, t, n_sb, HEAD_DIM), new_conv_p[None],
            bkp.reshape(1, b, keep, n_c, HEAD_DIM), bvp.reshape(1, b, keep, n_c, HEAD_DIM),
            ks.reshape(1, bs, ts, n_sb, HEAD_DIM), vs.reshape(1, bs, ts, n_sb, HEAD_DIM), new_conv_s[None],
            bks.reshape(1, bs, ts, n_c, HEAD_DIM), bvs.reshape(1, bs, ts, n_c, HEAD_DIM))
```

```python
import functools

import jax
import jax.numpy as jnp
from jax import lax
from jax.experimental import pallas as pl
from jax.experimental.pallas import tpu as pltpu

F32 = jnp.float32
BF16 = jnp.bfloat16

HEAD_DIM = 64
CHUNK = 64
CHUNK_SHIFT = 6
LEFT_CHUNKS = 8
BAND_PAST = LEFT_CHUNKS * CHUNK
REL_CLIP = 128
CONV_WIDTH = 31
CONV_STATE = CONV_WIDTH - 1
RMS_EPS = 1e-6
LN_EPS = 1e-5
LOG2E = 1.4426950408889634
SCALE = HEAD_DIM ** -0.5

V7X_LANES = 128
SUBLANES = 8
V7X_MXU_DIM = 256
V7X_VMEM_LIMIT = 56 * 1024 * 1024
ROW_TILE = 512
FF_CHUNK = 512
HALO_ROWS = 32
CONV_ROWS = 64
PAIRS_PER_STEP = 4
BAND_Q = 256
BAND_WIN = BAND_PAST + BAND_Q
REL_FIRST = REL_CLIP + 1 - CHUNK
REL_COLS = 256
BIAS_UNROLL = 4


def _params(*semantics):
    return pltpu.CompilerParams(dimension_semantics=semantics, vmem_limit_bytes=V7X_VMEM_LIMIT)


def _dot(a, b):
    return jnp.dot(a, b, preferred_element_type=F32)


def _dot_t(a, b):
    return lax.dot_general(a, b, (((1,), (1,)), ((), ())), preferred_element_type=F32)


def _rms_rows(x, g):
    r = lax.rsqrt(jnp.mean(x * x, axis=-1, keepdims=True) + RMS_EPS)
    return (x * r) * g


def _resident(shape):
    return pl.BlockSpec(shape, lambda *_: (0,) * len(shape), pipeline_mode=pl.Buffered(1))


def _lane_lo():
    return lax.broadcasted_iota(jnp.int32, (1, V7X_LANES), 1) < HEAD_DIM


def _split_heads(xp, lane_lo):
    zero = jnp.zeros_like(xp)
    return jnp.concatenate([jnp.where(lane_lo, xp, zero), jnp.where(lane_lo, zero, xp)], axis=0)


def _pair(p):
    return slice(p * V7X_LANES, (p + 1) * V7X_LANES)


def _store_heads(dst_ref, src_ref):
    nh = dst_ref.shape[1]
    per_head = jnp.stack([src_ref[:, h * HEAD_DIM:(h + 1) * HEAD_DIM] for h in range(nh)], axis=0)
    dst_ref[...] = jnp.swapaxes(per_head, 0, 1)


def _ab_in_kernel(x_ref, g_ref, w_ref, q_ref, kb_ref, vb_ref, k_ref, v_ref, u_ref, h_ref, kf_ref, vf_ref):
    h_ref[...] = _rms_rows(x_ref[...], g_ref[...]).astype(BF16)
    sw = q_ref.shape[-1]
    cw = u_ref.shape[-1]
    proj = lambda c0, n: _dot(h_ref[...], w_ref[:, c0:c0 + n])
    q_ref[...] = (proj(0, sw) * SCALE).astype(BF16)
    for c0, b_ref, o_ref, f_ref in ((sw, kb_ref, k_ref, kf_ref), (2 * sw, vb_ref, v_ref, vf_ref)):
        f_ref[...] = proj(c0, sw)
        b_ref[...] = f_ref[...].astype(BF16)
        _store_heads(o_ref, f_ref)
    u_ref[...] = proj(3 * sw, cw) * jax.nn.sigmoid(proj(3 * sw + cw, cw))


def _ab_in(x, g, w, sw, cw, tm):
    m, d = x.shape
    nh = sw // HEAD_DIM
    row = lambda n: pl.BlockSpec((tm, n), lambda i: (i, 0))
    heads = pl.BlockSpec((tm, nh, HEAD_DIM), lambda i: (i, 0, 0))
    return pl.pallas_call(
        _ab_in_kernel,
        grid=(m // tm,),
        in_specs=[row(d), _resident((1, d)), _resident(w.shape)],
        out_specs=[row(sw)] * 3 + [heads] * 2 + [row(cw)],
        out_shape=[jax.ShapeDtypeStruct((m, sw), BF16)] * 3 + [jax.ShapeDtypeStruct((m, nh, HEAD_DIM), F32)] * 2
        + [jax.ShapeDtypeStruct((m, cw), F32)],
        scratch_shapes=[pltpu.VMEM((tm, d), BF16)] + [pltpu.VMEM((tm, sw), F32)] * 2,
        compiler_params=_params("arbitrary"),
        name="ab_in_proj",
    )(x, g, w)


def _softplus(z):
    return jnp.maximum(z, 0.0) + jnp.log(1.0 + jnp.exp2(jnp.abs(z) * -LOG2E))


def _sb_scores(q2s, kblks):
    return [_dot_t(q2, kblk) for q2, kblk in zip(q2s, kblks)]


def _sb_cumsums(zs, tri2, mask):
    cs = []
    for z in zs:
        sp = _softplus(z)
        if mask is not None:
            sp = jnp.where(mask, sp, 0.0)
        hi = sp.astype(BF16)
        lo = (sp - hi.astype(F32)).astype(BF16)
        cs.append(_dot(jnp.concatenate([hi, lo], axis=1), tri2))
    return cs


def _sb_weigh(zs, cs, vblks, states, mask, lane_lo):
    out = []
    for z, c, vblk, (carry, acc) in zip(zs, cs, vblks, states):
        bq = acc.shape[0]
        w = jnp.exp2((z - c - carry) * LOG2E)
        if mask is not None:
            w = jnp.where(mask, w, 0.0)
        w = w.astype(BF16)
        wcat = jnp.concatenate([w[:bq], w[bq:]], axis=1)
        out.append((carry + c[:, 0:1], acc + _dot(wcat, _split_heads(vblk, lane_lo))))
    return tuple(out)


def _sb_blocks(q2s, kblks, vblks, tri2, states, mask, lane_lo):
    zs = _sb_scores(q2s, kblks)
    return _sb_weigh(zs, _sb_cumsums(zs, tri2, mask), vblks, states, mask, lane_lo)


def _sb_consts(bq, kb):
    tri = jnp.where((lax.broadcasted_iota(jnp.int32, (2 * kb, kb), 0) & (kb - 1))
                    >= lax.broadcasted_iota(jnp.int32, (2 * kb, kb), 1), 1.0, 0.0).astype(BF16)
    assert bq & (bq - 1) == 0
    t_loc = lax.broadcasted_iota(jnp.int32, (2 * bq, kb), 0) & (bq - 1)
    causal = lax.broadcasted_iota(jnp.int32, (2 * bq, kb), 1) < t_loc
    return _lane_lo(), tri, causal


def _sb_zero_state(bq):
    return jnp.zeros((2 * bq, 1), F32), jnp.zeros((bq, V7X_LANES), F32)


def _sb_prompt_kernel(q_ref, k_ref, v_ref, o_ref):
    i = pl.program_id(2)
    bq = q_ref.shape[1]
    npair = q_ref.shape[2] // V7X_LANES
    lane_lo, tri, causal = _sb_consts(bq, bq)
    q2 = [_split_heads(q_ref[0, :, _pair(p)], lane_lo) for p in range(npair)]

    def blocks(j):
        rows = pl.ds(pl.multiple_of(j * bq, bq), bq)
        return ([k_ref[0, rows, _pair(p)] for p in range(npair)], [v_ref[0, rows, _pair(p)] for p in range(npair)])

    def sweep(j, states, mask):
        return _sb_blocks(q2, *blocks(j), tri, states, mask, lane_lo)

    def sweep2(s, states):
        j = i - 1 - 2 * s
        (ka, va), (kb, vb) = blocks(j), blocks(j - 1)
        z_a = _sb_scores(q2, ka)
        z_b = _sb_scores(q2, kb)
        c_a = _sb_cumsums(z_a, tri, None)
        c_b = _sb_cumsums(z_b, tri, None)
        states = _sb_weigh(z_a, c_a, va, states, None, lane_lo)
        return _sb_weigh(z_b, c_b, vb, states, None, lane_lo)

    states = sweep(i, (_sb_zero_state(bq),) * npair, causal)
    states = lax.fori_loop(0, i // 2, sweep2, states)
    states = lax.cond(i % 2 == 1, lambda st: sweep(0, st, None), lambda st: st, states)
    for p in range(npair):
        o_ref[0, :, _pair(p)] = states[p][1].astype(BF16)


def _sb_prompt(q, k, v):
    b, t, sw = q.shape
    bq = V7X_MXU_DIM
    width = PAIRS_PER_STEP * V7X_LANES
    qspec = pl.BlockSpec((1, bq, width), lambda bi, p, i: (bi, i, p))
    kvspec = pl.BlockSpec((1, t, width), lambda bi, p, i: (bi, 0, p))
    return pl.pallas_call(
        _sb_prompt_kernel,
        grid=(b, sw // width, t // bq),
        in_specs=[qspec, kvspec, kvspec],
        out_specs=qspec,
        out_shape=jax.ShapeDtypeStruct((b, t, sw), BF16),
        compiler_params=_params("arbitrary", "arbitrary", "arbitrary"),
        name="sb_attn_prompt",
    )(q, k, v)


def _pair_rows(ref, p):
    return jnp.concatenate([ref[0, 2 * p], ref[0, 2 * p + 1]], axis=0).astype(BF16)


def _split_heads_t(xt, row_lo):
    zero = jnp.zeros_like(xt)
    return jnp.concatenate([jnp.where(row_lo, xt, zero), jnp.where(row_lo, zero, xt)], axis=1)


def _row_lo():
    return lax.broadcasted_iota(jnp.int32, (V7X_LANES, 1), 0) < HEAD_DIM


def _sb_sample_kernel(q_ref, kc_ref, vc_ref, kn_ref, vn_ref, o_ref, kpad_ref, vpad_ref):
    bq = q_ref.shape[1]
    kb = kpad_ref.shape[0]
    npair = q_ref.shape[2] // V7X_LANES
    nblk = kc_ref.shape[3] // kb
    lane_lo, tri2, causal = _sb_consts(bq, kb)
    row_lo = _row_lo()
    kpad_ref[...] = jnp.zeros_like(kpad_ref)
    vpad_ref[...] = jnp.zeros_like(vpad_ref)
    kpad_ref[0:bq, :] = kn_ref[0]
    vpad_ref[0:bq, :] = vn_ref[0]
    cols = [slice(j * kb, (j + 1) * kb) for j in reversed(range(nblk))]
    for p in range(npair):
        q2 = _split_heads(q_ref[0, :, _pair(p)], lane_lo)
        kt, vt = _pair_rows(kc_ref, p), _pair_rows(vc_ref, p)
        z = [_dot_t(q2, kpad_ref[:, _pair(p)])] + [_dot(q2, kt[:, c]) for c in cols]
        sp = [_softplus(zb) for zb in z]
        sp[0] = jnp.where(causal, sp[0], 0.0)
        sp_all = jnp.concatenate(sp, axis=0)
        hi = sp_all.astype(BF16)
        lo = (sp_all - hi.astype(F32)).astype(BF16)
        c_all = _dot(jnp.concatenate([hi, lo], axis=1), tri2)
        carry = jnp.zeros((2 * bq, 1), F32)
        ws = []
        for j, zb in enumerate(z):
            c = c_all[j * 2 * bq:(j + 1) * 2 * bq]
            w = jnp.exp2((zb - c - carry) * LOG2E)
            if j == 0:
                w = jnp.where(causal, w, 0.0)
            carry = carry + c[:, 0:1]
            w = w.astype(BF16)
            ws.append(jnp.concatenate([w[:bq], w[bq:]], axis=1))
        vt_cat = jnp.concatenate([_split_heads_t(vt[:, c], row_lo) for c in cols], axis=1)
        acc = _dot(ws[0], _split_heads(vpad_ref[:, _pair(p)], lane_lo)) + _dot_t(jnp.concatenate(ws[1:], axis=1), vt_cat)
        o_ref[0, :, _pair(p)] = acc.astype(BF16)


def _sb_sample(q, kc, vc, kn, vn):
    b, t, sw = q.shape
    new = pl.BlockSpec((1, t, sw), lambda bi: (bi, 0, 0))
    cache = pl.BlockSpec((1,) + kc.shape[1:], lambda bi: (bi, 0, 0, 0))
    return pl.pallas_call(
        _sb_sample_kernel,
        grid=(b,),
        in_specs=[new, cache, cache, new, new],
        out_specs=new,
        out_shape=jax.ShapeDtypeStruct((b, t, sw), BF16),
        scratch_shapes=[pltpu.VMEM((V7X_MXU_DIM, sw), BF16)] * 2,
        compiler_params=_params("arbitrary"),
        name="sb_attn_sample",
    )(q, kc, vc, kn, vn)


def _conv_tile(u_ref, halo, dw_ref, b_ref, g_ref, be_ref, c_ref, ext_ref, sh_ref):
    tm = u_ref.shape[0]
    ext_ref[0:HALO_ROWS, :] = halo
    ext_ref[HALO_ROWS:, :] = u_ref[...]
    n = sh_ref.shape[1]
    for s in range(1, SUBLANES):
        sh_ref[s - 1] = ext_ref[s:s + n, :]
    first = HALO_ROWS - CONV_STATE
    rows = min(CONV_ROWS, tm)
    for r0 in range(0, tm, rows):
        y = b_ref[...]
        for w in range(CONV_WIDTH):
            s = (first + w) % SUBLANES
            a = r0 + first + w - s
            tap = ext_ref[a:a + rows, :] if s == 0 else sh_ref[s - 1, a:a + rows, :]
            y = y + tap * dw_ref[w:w + 1, :]
        mu = jnp.mean(y, axis=-1, keepdims=True)
        var = jnp.mean(jnp.square(y - mu), axis=-1, keepdims=True)
        yn = ((y - mu) * lax.rsqrt(var + LN_EPS)) * g_ref[...] + be_ref[...]
        c_ref[r0:r0 + rows, :] = (yn * jax.nn.sigmoid(yn)).astype(BF16)


def _conv_scratch(tm, c):
    return [pltpu.VMEM((HALO_ROWS + tm, c), F32), pltpu.VMEM((SUBLANES - 1, HALO_ROWS + tm - SUBLANES, c), F32)]


def _conv_kernel(u_ref, halo_ref, dw_ref, b_ref, g_ref, be_ref, c_ref, ext_ref, sh_ref, *, halo_is_history):
    halo = halo_ref[0]
    if not halo_is_history:
        halo = jnp.where(pl.program_id(1) > 0, halo, 0.0)
    _conv_tile(u_ref.at[0], halo, dw_ref, b_ref, g_ref, be_ref, c_ref.at[0], ext_ref, sh_ref)


def _conv(u, halo_src, dw, b, g, be, tm, halo_is_history):
    bsz, t, c = u.shape
    if halo_is_history:
        halo_spec = pl.BlockSpec((1, HALO_ROWS, c), lambda bi, ti: (bi, 0, 0))
    else:
        per = tm // HALO_ROWS
        halo_spec = pl.BlockSpec((1, HALO_ROWS, c), lambda bi, ti: (bi, jnp.maximum(ti * per - 1, 0), 0))
    tile = pl.BlockSpec((1, tm, c), lambda bi, ti: (bi, ti, 0))
    vec = _resident((1, c))
    return pl.pallas_call(
        functools.partial(_conv_kernel, halo_is_history=halo_is_history),
        grid=(bsz, t // tm),
        in_specs=[tile, halo_spec, _resident(dw.shape), vec, vec, vec],
        out_specs=tile,
        out_shape=jax.ShapeDtypeStruct((bsz, t, c), BF16),
        scratch_shapes=_conv_scratch(tm, c),
        compiler_params=_params("arbitrary", "arbitrary"),
        name="conv_module",
    )(u, halo_src, dw, b, g, be)


def _mlp_body(x1, g_ref, wu_ref, wd_ref, gf_ref, y_ref, h_ref, final_norm):
    h_ref[...] = _rms_rows(x1, g_ref[...]).astype(BF16)
    y_ref[...] = x1
    for c0 in range(0, wu_ref.shape[1], FF_CHUNK):
        up = _dot(h_ref[...], wu_ref[:, c0:c0 + FF_CHUNK])
        act = jnp.square(jnp.maximum(up, 0.0)).astype(BF16)
        y_ref[...] += _dot(act, wd_ref[c0:c0 + FF_CHUNK, :])
    if final_norm:
        y_ref[...] = _rms_rows(y_ref[...], gf_ref[...])


def _out_mlp_kernel(x_ref, m0_ref, m1_ref, wo_ref, g_ref, wu_ref, wd_ref, gf_ref, y_ref, h_ref, *, final_norm):
    half = m0_ref.shape[-1]
    x1 = x_ref[...] + _dot(m0_ref[...], wo_ref[0:half, :]) + _dot(m1_ref[...], wo_ref[half:, :])
    _mlp_body(x1, g_ref, wu_ref, wd_ref, gf_ref, y_ref, h_ref, final_norm)


def _out_mlp(x, m0, m1, m1_block, wo, g, wu, wd, gf, tm, final_norm):
    m, d = x.shape
    half = wo.shape[0] // 2
    row = pl.BlockSpec((tm, d), lambda i: (i, 0))
    return pl.pallas_call(
        functools.partial(_out_mlp_kernel, final_norm=final_norm),
        grid=(m // tm,),
        in_specs=[row, pl.BlockSpec((tm, half), lambda i: (i, 0)), pl.BlockSpec((tm, half), lambda i: (i, m1_block)),
                  _resident(wo.shape), _resident((1, d)), _resident(wu.shape), _resident(wd.shape), _resident((1, d))],
        out_specs=row,
        out_shape=jax.ShapeDtypeStruct((m, d), F32),
        scratch_shapes=[pltpu.VMEM((tm, d), BF16)],
        compiler_params=_params("arbitrary"),
        name="out_proj_mlp",
    )(x, m0, m1, wo, g, wu, wd, gf)


def _conv_out_mlp_kernel(x_ref, a_ref, u_ref, halo_ref, dw_ref, cb_ref, cg_ref, cbe_ref, wo_ref, g_ref, wu_ref, wd_ref,
                         y_ref, h_ref, c_ref, ext_ref, sh_ref, *, tiles_per_stream):
    i = pl.program_id(0)
    half = a_ref.shape[-1]

    @pl.when(i == 0)
    def _():
        c_ref[...] = jnp.zeros_like(c_ref)

    x1 = x_ref[...] + _dot(a_ref[...], wo_ref[0:half, :]) + _dot(c_ref[...], wo_ref[half:, :])
    _mlp_body(x1, g_ref, wu_ref, wd_ref, None, y_ref, h_ref, False)
    halo = jnp.where(i % tiles_per_stream != 0, halo_ref[...], 0.0)
    _conv_tile(u_ref, halo, dw_ref, cb_ref, cg_ref, cbe_ref, c_ref, ext_ref, sh_ref)


def _conv_out_mlp(x, a, u, dw, cb, cg, cbe, wo, g, wu, wd, t, tm):
    m, d = x.shape
    half = wo.shape[0] // 2
    c = u.shape[1]
    n = m // tm
    prev = lambda w: pl.BlockSpec((tm, w), lambda i: (jnp.maximum(i - 1, 0), 0))
    cur = pl.BlockSpec((tm, c), lambda i: (jnp.minimum(i, n - 1), 0))
    per = tm // HALO_ROWS
    halo = pl.BlockSpec((HALO_ROWS, c), lambda i: (jnp.maximum(jnp.minimum(i, n - 1) * per - 1, 0), 0))
    vec = _resident((1, c))
    return pl.pallas_call(
        functools.partial(_conv_out_mlp_kernel, tiles_per_stream=t // tm),
        grid=(n + 1,),
        in_specs=[prev(d), prev(half), cur, halo, _resident(dw.shape), vec, vec, vec,
                  _resident(wo.shape), _resident((1, d)), _resident(wu.shape), _resident(wd.shape)],
        out_specs=prev(d),
        out_shape=jax.ShapeDtypeStruct((m, d), F32),
        scratch_shapes=[pltpu.VMEM((tm, d), BF16), pltpu.VMEM((tm, c), BF16)] + _conv_scratch(tm, c),
        compiler_params=_params("arbitrary"),
        name="conv_out_proj_mlp",
    )(x, a, u, u, dw, cb, cg, cbe, wo, g, wu, wd)


def _c_in_kernel(x_ref, g_ref, w_ref, q_ref, k_ref, v_ref, kt_ref, vt_ref, h_ref, kf_ref, vf_ref, *, tail_every):
    h_ref[...] = _rms_rows(x_ref[...], g_ref[...]).astype(BF16)
    cw = q_ref.shape[-1]
    half = cw // 2
    for c0 in range(0, cw, half):
        cols = slice(c0, c0 + half)
        q_ref[:, cols] = (_dot(h_ref[...], w_ref[:, c0:c0 + half]) * SCALE).astype(BF16)
        for base, b_ref, f_ref in ((cw, k_ref, kf_ref), (2 * cw, v_ref, vf_ref)):
            f_ref[:, cols] = _dot(h_ref[...], w_ref[:, base + c0:base + c0 + half])
            b_ref[:, cols] = f_ref[:, cols].astype(BF16)

    @pl.when((pl.program_id(0) + 1) % tail_every == 0)
    def _():
        _store_heads(kt_ref, kf_ref)
        _store_heads(vt_ref, vf_ref)


def _c_in(x, g, w, t, keep, tm):
    m, d = x.shape
    cw = w.shape[1] // 3
    nh = cw // HEAD_DIM
    row = lambda n: pl.BlockSpec((tm, n), lambda i: (i, 0))
    if keep == t:
        tail_every = 1
        tail_map = lambda i: (i, 0, 0)
    else:
        assert keep == tm and t % tm == 0
        tail_every = t // tm
        tail_map = lambda i: (i // tail_every, 0, 0)
    tail = pl.BlockSpec((tm, nh, HEAD_DIM), tail_map)
    nb = m // t
    return pl.pallas_call(
        functools.partial(_c_in_kernel, tail_every=tail_every),
        grid=(m // tm,),
        in_specs=[row(d), _resident((1, d)), _resident(w.shape)],
        out_specs=[row(cw), row(cw), row(cw), tail, tail],
        out_shape=[jax.ShapeDtypeStruct((m, cw), BF16)] * 3 + [jax.ShapeDtypeStruct((nb * keep, nh, HEAD_DIM), F32)] * 2,
        scratch_shapes=[pltpu.VMEM((tm, d), BF16)] + [pltpu.VMEM((tm, cw), F32)] * 2,
        compiler_params=_params("arbitrary"),
        name="c_in_proj",
    )(x, g, w)


def _bias_kernel(rb_ref, o_ref):
    rb = rb_ref[...]
    nh = rb.shape[0]
    hi = rb.astype(BF16)
    r1 = rb - hi.astype(F32)
    mid = r1.astype(BF16)
    lo = (r1 - mid.astype(F32)).astype(BF16)
    terms = jnp.concatenate([hi, mid, lo], axis=0)
    ncol, win = rb.shape[1], o_ref.shape[2]
    s = lax.broadcasted_iota(jnp.int32, (1, win), 1)
    kk = lax.broadcasted_iota(jnp.int32, (ncol, win), 0)

    def body(r, carry):
        idx = jnp.clip(BAND_PAST + r - s, -REL_CLIP, REL_CLIP) + REL_CLIP - REL_FIRST
        kc, qc = s >> CHUNK_SHIFT, r >> CHUNK_SHIFT
        visible = (kc >= qc) & (kc <= qc + LEFT_CHUNKS)
        onehot = jnp.where((kk == idx) & visible, 1.0, 0.0).astype(BF16)
        parts = _dot(terms, onehot)
        row = (parts[0:nh] + parts[nh:2 * nh]) + parts[2 * nh:3 * nh]
        o_ref[r] = jnp.where(visible, row, -jnp.inf)
        return carry

    lax.fori_loop(0, o_ref.shape[0], body, 0, unroll=BIAS_UNROLL)


def _bias_tile(rel_bias):
    h = rel_bias.shape[0]
    cols = rel_bias[:, REL_FIRST:]
    cols = jnp.pad(cols, ((0, 0), (0, REL_COLS - cols.shape[1])))
    tile = pl.pallas_call(
        _bias_kernel,
        out_shape=jax.ShapeDtypeStruct((BAND_Q, h, BAND_WIN), F32),
        compiler_params=pltpu.CompilerParams(vmem_limit_bytes=V7X_VMEM_LIMIT),
        name="band_bias_tile",
    )(cols)
    return jnp.transpose(tile, (1, 0, 2))


def _band_core(q_pairs, kws, vws, biases, valid, lane_lo):
    bq = q_pairs[0].shape[0]
    zs = [_dot_t(_split_heads(q_pair, lane_lo), kw) for q_pair, kw in zip(q_pairs, kws)]
    es, invs = [], []
    for z, (bias0, bias1) in zip(zs, biases):
        s = z + jnp.concatenate([bias0, bias1], axis=0)
        s = jnp.where(valid, s, -jnp.inf)
        e = jnp.exp2((s - jnp.max(s, axis=-1, keepdims=True)) * LOG2E)
        invs.append(1.0 / jnp.sum(e, axis=-1, keepdims=True))
        es.append(e.astype(BF16))
    outs = []
    for e, inv, vw in zip(es, invs, vws):
        o = _dot(jnp.concatenate([e[:bq], e[bq:]], axis=1), _split_heads(vw, lane_lo))
        outs.append((o * jnp.where(lane_lo, inv[:bq], inv[bq:])).astype(BF16))
    return outs


def _band_prompt_kernel(q_ref, k_ref, v_ref, bias_ref, o_ref, kpad_ref, vpad_ref):
    i = pl.program_id(2)
    bq = q_ref.shape[1]
    npair = q_ref.shape[2] // V7X_LANES
    win = bias_ref.shape[2]
    past = win - bq

    @pl.when(i == 0)
    def _():
        kpad_ref[0:past, :] = jnp.zeros((past, kpad_ref.shape[1]), BF16)
        vpad_ref[0:past, :] = jnp.zeros((past, vpad_ref.shape[1]), BF16)
        kpad_ref[past:, :] = k_ref[0]
        vpad_ref[past:, :] = v_ref[0]

    rows = pl.ds(pl.multiple_of(i * bq, bq), win)
    valid = lax.broadcasted_iota(jnp.int32, (1, win), 1) >= past - i * bq
    lane_lo = _lane_lo()
    pairs = range(npair)
    outs = _band_core([q_ref[0, :, _pair(p)] for p in pairs], [kpad_ref[rows, _pair(p)] for p in pairs],
                      [vpad_ref[rows, _pair(p)] for p in pairs], [(bias_ref[2 * p], bias_ref[2 * p + 1]) for p in pairs],
                      valid, lane_lo)
    for p in pairs:
        o_ref[0, :, _pair(p)] = outs[p]


def _band_prompt(q, k, v, bias):
    b, t, cw = q.shape
    bq, win = bias.shape[1], bias.shape[2]
    width = PAIRS_PER_STEP * V7X_LANES
    qspec = pl.BlockSpec((1, bq, width), lambda p, bi, i: (bi, i, p))
    kvspec = pl.BlockSpec((1, t, width), lambda p, bi, i: (bi, 0, p))
    return pl.pallas_call(
        _band_prompt_kernel,
        grid=(cw // width, b, t // bq),
        in_specs=[qspec, kvspec, kvspec, pl.BlockSpec((2 * PAIRS_PER_STEP, bq, win), lambda p, bi, i: (p, 0, 0))],
        out_specs=qspec,
        out_shape=jax.ShapeDtypeStruct((b, t, cw), BF16),
        scratch_shapes=[pltpu.VMEM((win - bq + t, width), BF16)] * 2,
        compiler_params=_params("arbitrary", "arbitrary", "arbitrary"),
        name="band_attn_prompt",
    )(q, k, v, bias)


def _band_sample_kernel(q_ref, kc_ref, vc_ref, kn_ref, vn_ref, bias_ref, o_ref, kpad_ref, vpad_ref):
    t = q_ref.shape[1]
    npair = q_ref.shape[2] // V7X_LANES
    past = kc_ref.shape[3]
    win = bias_ref.shape[2]
    kpad_ref[...] = jnp.zeros_like(kpad_ref)
    vpad_ref[...] = jnp.zeros_like(vpad_ref)
    kpad_ref[0:t, :] = kn_ref[0]
    vpad_ref[0:t, :] = vn_ref[0]
    valid = lax.broadcasted_iota(jnp.int32, (1, win), 1) < past + t
    lane_lo, row_lo = _lane_lo(), _row_lo()
    pairs = range(npair)
    q2s = [_split_heads(q_ref[0, :, _pair(p)], lane_lo) for p in pairs]
    zs = [jnp.concatenate([_dot(q2s[p], _pair_rows(kc_ref, p)), _dot_t(q2s[p], kpad_ref[:, _pair(p)])], axis=1)
          for p in pairs]
    for p in pairs:
        s = zs[p] + jnp.concatenate([bias_ref[2 * p], bias_ref[2 * p + 1]], axis=0)
        s = jnp.where(valid, s, -jnp.inf)
        e = jnp.exp2((s - jnp.max(s, axis=-1, keepdims=True)) * LOG2E)
        inv = 1.0 / jnp.sum(e, axis=-1, keepdims=True)
        e = e.astype(BF16)
        e_cache = jnp.concatenate([e[:t, :past], e[t:, :past]], axis=1)
        e_new = jnp.concatenate([e[:t, past:], e[t:, past:]], axis=1)
        o = (_dot_t(e_cache, _split_heads_t(_pair_rows(vc_ref, p), row_lo))
             + _dot(e_new, _split_heads(vpad_ref[:, _pair(p)], lane_lo)))
        o_ref[0, :, _pair(p)] = (o * jnp.where(lane_lo, inv[:t], inv[t:])).astype(BF16)


def _band_sample(q, kc, vc, kn, vn, bias):
    b, t, cw = q.shape
    nh = bias.shape[0]
    past, win = kc.shape[3], bias.shape[2]
    new = pl.BlockSpec((1, t, cw), lambda bi: (bi, 0, 0))
    cache = pl.BlockSpec((1,) + kc.shape[1:], lambda bi: (bi, 0, 0, 0))
    return pl.pallas_call(
        _band_sample_kernel,
        grid=(b,),
        in_specs=[new, cache, cache, new, new, pl.BlockSpec((nh, t, win), lambda bi: (0, 0, 0))],
        out_specs=new,
        out_shape=jax.ShapeDtypeStruct((b, t, cw), BF16),
        scratch_shapes=[pltpu.VMEM((win - past, cw), BF16)] * 2,
        compiler_params=_params("arbitrary"),
        name="band_attn_sample",
    )(q, kc, vc, kn, vn, bias)


def kernel(x_prompt, x_sample, cache_sb_k, cache_sb_v, cache_conv, cache_band_k, cache_band_v, norm_mix, norm_ffn,
           norm_final, w_in_ab, w_out_ab, dw_w, dw_b, conv_ln_g, conv_ln_b, w_in_c, w_out_c, rel_bias, w_up, w_down):
    b, t, d = x_prompt.shape
    bs, ts, _ = x_sample.shape
    n_sb, n_c = cache_sb_k.shape[3], cache_band_k.shape[3]
    sw, cc, cw = n_sb * HEAD_DIM, dw_w.shape[2], n_c * HEAD_DIM
    past = cache_sb_k.shape[2]
    band_past = cache_band_k.shape[2]
    keep = min(BAND_PAST, t)
    assert w_in_ab.shape[0] == 1 and w_in_c.shape[0] == 1 and norm_mix.shape[0] == 2
    assert sw == cc and sw + cc == d and cw == d and w_in_ab.shape[2] == 3 * sw + 2 * cc
    assert t % ROW_TILE == 0 and ts % 16 == 0 and ts <= V7X_MXU_DIM
    assert past % V7X_MXU_DIM == 0 and band_past == BAND_PAST and band_past + ts <= BAND_WIN

    tm_p, tm_s = ROW_TILE, bs * ts
    vec = lambda a: a.reshape(1, -1)
    w_in0, w_out0 = w_in_ab[0].astype(BF16), w_out_ab[0].astype(BF16)
    w_in1, w_out1 = w_in_c[0].astype(BF16), w_out_c[0].astype(BF16)
    w_up_b, w_down_b = w_up.astype(BF16), w_down.astype(BF16)
    dw_pad = jnp.pad(dw_w[0], ((0, HALO_ROWS - CONV_WIDTH), (0, 0)))
    conv_vecs = (vec(dw_b[0]), vec(conv_ln_g[0]), vec(conv_ln_b[0]))
    bias = _bias_tile(rel_bias[0])

    def layer0(x, tm, attn, conv):
        q, kb, vb, k, v, u = _ab_in(x, vec(norm_mix[0]), w_in0, sw, cc, tm)
        a = attn(q, kb, vb)
        if conv is None:
            x2 = _conv_out_mlp(x, a, u, dw_pad, *conv_vecs, w_out0, vec(norm_ffn[0]), w_up_b[0], w_down_b[0], t, tm)
        else:
            x2 = _out_mlp(x, a, conv(u), 0, w_out0, vec(norm_ffn[0]), w_up_b[0], w_down_b[0], vec(norm_final), tm, False)
        return x2, k, v, u

    def layer1(x, tm, tlen, tail, attn):
        q, k, v, kt, vt = _c_in(x, vec(norm_mix[1]), w_in1, tlen, tail, tm)
        o = attn(q, k, v)
        y = _out_mlp(x, o, o, 1, w_out1, vec(norm_ffn[1]), w_up_b[1], w_down_b[1], vec(norm_final), tm, True)
        return y, kt, vt

    shp = lambda z, n: z.reshape(b, t, n)
    xp2, kp, vp, up = layer0(
        x_prompt.reshape(b * t, d), tm_p,
        lambda q, k, v: _sb_prompt(shp(q, sw), shp(k, sw), shp(v, sw)).reshape(b * t, sw), None)
    yp, bkp, bvp = layer1(
        xp2, tm_p, t, keep,
        lambda q, k, v: _band_prompt(shp(q, cw), shp(k, cw), shp(v, cw), bias).reshape(b * t, cw))

    shs = lambda z, n: z.reshape(bs, ts, n)
    conv_hist = jnp.pad(cache_conv[0], ((0, 0), (HALO_ROWS - CONV_STATE, 0), (0, 0)))
    by_head = lambda c: jnp.transpose(c[0], (0, 2, 3, 1))
    ck, cv = by_head(cache_sb_k), by_head(cache_sb_v)
    xs2, ks, vs, us = layer0(
        x_sample.reshape(bs * ts, d), tm_s,
        lambda q, k, v: _sb_sample(shs(q, sw), ck, cv, shs(k, sw), shs(v, sw)).reshape(bs * ts, sw),
        lambda u: _conv(shs(u, cc), conv_hist, dw_pad, *conv_vecs, ts, True).reshape(bs * ts, cc))
    bck, bcv = by_head(cache_band_k), by_head(cache_band_v)
    ys, bks, bvs = layer1(
        xs2, tm_s, ts, ts,
        lambda q, k, v: _band_sample(shs(q, cw), bck, bcv, shs(k, cw), shs(v, cw), bias).reshape(bs * ts, cw))

    new_conv_p = up.reshape(b, t, cc)[:, t - CONV_STATE:]
    new_conv_s = jnp.concatenate([cache_conv[0], us.reshape(bs, ts, cc)], axis=1)[:, ts:]
    return (yp.reshape(b, t, d), ys.reshape(bs, ts, d),
            kp.reshape(1, b, t, n_sb, HEAD_DIM), vp.reshape(1, b, t, n_sb, HEAD_DIM), new_conv_p[None],
            bkp.reshape(1, b, keep, n_c, HEAD_DIM), bvp.reshape(1, b, keep, n_c, HEAD_DIM),
            ks.reshape(1, bs, ts, n_sb, HEAD_DIM), vs.reshape(1, bs, ts, n_sb, HEAD_DIM), new_conv_s[None],
            bks.reshape(1, bs, ts, n_c, HEAD_DIM), bvs.reshape(1, bs, ts, n_c, HEAD_DIM))
```

```python
import functools

import jax
import jax.numpy as jnp
from jax import lax
from jax.experimental import pallas as pl
from jax.experimental.pallas import tpu as pltpu

F32 = jnp.float32
BF16 = jnp.bfloat16

HEAD_DIM = 64
CHUNK = 64
CHUNK_SHIFT = 6
LEFT_CHUNKS = 8
BAND_PAST = LEFT_CHUNKS * CHUNK
REL_CLIP = 128
CONV_WIDTH = 31
CONV_STATE = CONV_WIDTH - 1
RMS_EPS = 1e-6
LN_EPS = 1e-5
LOG2E = 1.4426950408889634
SCALE = HEAD_DIM ** -0.5

V7X_LANES = 128
SUBLANES = 8
V7X_MXU_DIM = 256
V7X_VMEM_LIMIT = 56 * 1024 * 1024
ROW_TILE = 512
FF_CHUNK = 512
CONV_TILE = 256
HALO_ROWS = 32
CONV_ROWS = 64
PAIRS_PER_STEP = 4
BAND_Q = 256
BAND_WIN = BAND_PAST + BAND_Q
BAND_BLOCKS_PER_STEP = 2
REL_FIRST = REL_CLIP + 1 - CHUNK
REL_COLS = 256
BIAS_UNROLL = 4


def _params(*semantics):
    return pltpu.CompilerParams(dimension_semantics=semantics, vmem_limit_bytes=V7X_VMEM_LIMIT)


def _dot(a, b):
    return jnp.dot(a, b, preferred_element_type=F32)


def _dot_t(a, b):
    return lax.dot_general(a, b, (((1,), (1,)), ((), ())), preferred_element_type=F32)


def _rms_rows(x, g):
    r = lax.rsqrt(jnp.mean(x * x, axis=-1, keepdims=True) + RMS_EPS)
    return (x * r) * g


def _resident(shape):
    return pl.BlockSpec(shape, lambda *_: (0,) * len(shape), pipeline_mode=pl.Buffered(1))


def _lane_lo():
    return lax.broadcasted_iota(jnp.int32, (1, V7X_LANES), 1) < HEAD_DIM


def _row_lo():
    return lax.broadcasted_iota(jnp.int32, (V7X_LANES, 1), 0) < HEAD_DIM


def _split_heads(xp, lane_lo):
    zero = jnp.zeros_like(xp)
    return jnp.concatenate([jnp.where(lane_lo, xp, zero), jnp.where(lane_lo, zero, xp)], axis=0)


def _split_heads_t(xt, row_lo):
    zero = jnp.zeros_like(xt)
    return jnp.concatenate([jnp.where(row_lo, xt, zero), jnp.where(row_lo, zero, xt)], axis=1)


def _pair(p):
    return slice(p * V7X_LANES, (p + 1) * V7X_LANES)


def _store_heads(dst_ref, src_ref):
    nh = dst_ref.shape[1]
    per_head = jnp.stack([src_ref[:, h * HEAD_DIM:(h + 1) * HEAD_DIM] for h in range(nh)], axis=0)
    dst_ref[...] = jnp.swapaxes(per_head, 0, 1)


def _ab_in_kernel(x_ref, g_ref, w_ref, q_ref, kb_ref, vb_ref, k_ref, v_ref, u_ref, h_ref, kf_ref, vf_ref):
    h_ref[...] = _rms_rows(x_ref[...], g_ref[...]).astype(BF16)
    sw = q_ref.shape[-1]
    cw = u_ref.shape[-1]
    proj = lambda c0, n: _dot(h_ref[...], w_ref[:, c0:c0 + n])
    q_ref[...] = (proj(0, sw) * SCALE).astype(BF16)
    for c0, b_ref, o_ref, f_ref in ((sw, kb_ref, k_ref, kf_ref), (2 * sw, vb_ref, v_ref, vf_ref)):
        f_ref[...] = proj(c0, sw)
        b_ref[...] = f_ref[...].astype(BF16)
        _store_heads(o_ref, f_ref)
    u_ref[...] = proj(3 * sw, cw) * jax.nn.sigmoid(proj(3 * sw + cw, cw))


def _ab_in(x, g, w, sw, cw, tm):
    m, d = x.shape
    nh = sw // HEAD_DIM
    row = lambda n: pl.BlockSpec((tm, n), lambda i: (i, 0))
    heads = pl.BlockSpec((tm, nh, HEAD_DIM), lambda i: (i, 0, 0))
    return pl.pallas_call(
        _ab_in_kernel,
        grid=(m // tm,),
        in_specs=[row(d), _resident((1, d)), _resident(w.shape)],
        out_specs=[row(sw)] * 3 + [heads] * 2 + [row(cw)],
        out_shape=[jax.ShapeDtypeStruct((m, sw), BF16)] * 3 + [jax.ShapeDtypeStruct((m, nh, HEAD_DIM), F32)] * 2
        + [jax.ShapeDtypeStruct((m, cw), F32)],
        scratch_shapes=[pltpu.VMEM((tm, d), BF16)] + [pltpu.VMEM((tm, sw), F32)] * 2,
        compiler_params=_params("arbitrary"),
        name="ab_in_proj",
    )(x, g, w)


def _ab_in_t_kernel(x_ref, g_ref, w_ref, wkv_ref, q_ref, ktb_ref, vtb_ref, kt_ref, vt_ref, u_ref, h_ref):
    h_ref[...] = _rms_rows(x_ref[...], g_ref[...]).astype(BF16)
    sw = q_ref.shape[-1]
    cw = u_ref.shape[-1]
    proj = lambda c0, n: _dot(h_ref[...], w_ref[:, c0:c0 + n])
    q_ref[...] = (proj(0, sw) * SCALE).astype(BF16)
    for r0, b_ref, f_ref in ((0, ktb_ref, kt_ref), (sw, vtb_ref, vt_ref)):
        f_ref[0] = _dot_t(wkv_ref[r0:r0 + sw, :], h_ref[...])
        b_ref[0] = f_ref[0].astype(BF16)
    u_ref[...] = proj(3 * sw, cw) * jax.nn.sigmoid(proj(3 * sw + cw, cw))


def _ab_in_t(x, g, w, wkv_t, sw, cw, t, tm):
    m, d = x.shape
    per = t // tm
    row = lambda n: pl.BlockSpec((tm, n), lambda i: (i, 0))
    tr = pl.BlockSpec((1, sw, tm), lambda i: (i // per, 0, i % per))
    tshape = lambda dt: jax.ShapeDtypeStruct((m // t, sw, t), dt)
    return pl.pallas_call(
        _ab_in_t_kernel,
        grid=(m // tm,),
        in_specs=[row(d), _resident((1, d)), _resident(w.shape), _resident(wkv_t.shape)],
        out_specs=[row(sw), tr, tr, tr, tr, row(cw)],
        out_shape=[jax.ShapeDtypeStruct((m, sw), BF16), tshape(BF16), tshape(BF16), tshape(F32), tshape(F32),
                   jax.ShapeDtypeStruct((m, cw), F32)],
        scratch_shapes=[pltpu.VMEM((tm, d), BF16)],
        compiler_params=_params("arbitrary"),
        name="ab_in_proj_t",
    )(x, g, w, wkv_t)


def _softplus(z):
    return jnp.maximum(z, 0.0) + jnp.log(1.0 + jnp.exp2(jnp.abs(z) * -LOG2E))


def _sb_scores(q2s, ktblks):
    return [_dot(q2, ktblk) for q2, ktblk in zip(q2s, ktblks)]


def _sb_cumsums(zs, tri2, mask):
    cs = []
    for z in zs:
        sp = _softplus(z)
        if mask is not None:
            sp = jnp.where(mask, sp, 0.0)
        hi = sp.astype(BF16)
        lo = (sp - hi.astype(F32)).astype(BF16)
        cs.append(_dot(jnp.concatenate([hi, lo], axis=1), tri2))
    return cs


def _sb_weigh(zs, cs, vtblks, states, mask, row_lo):
    out = []
    for z, c, vtblk, (carry, acc) in zip(zs, cs, vtblks, states):
        bq = acc.shape[0]
        w = jnp.exp2((z - c - carry) * LOG2E)
        if mask is not None:
            w = jnp.where(mask, w, 0.0)
        w = w.astype(BF16)
        wcat = jnp.concatenate([w[:bq], w[bq:]], axis=1)
        out.append((carry + c[:, 0:1], acc + _dot_t(wcat, _split_heads_t(vtblk, row_lo))))
    return tuple(out)


def _sb_blocks(q2s, ktblks, vtblks, tri2, states, mask, row_lo):
    zs = _sb_scores(q2s, ktblks)
    return _sb_weigh(zs, _sb_cumsums(zs, tri2, mask), vtblks, states, mask, row_lo)


def _sb_consts(bq, kb):
    tri = jnp.where((lax.broadcasted_iota(jnp.int32, (2 * kb, kb), 0) & (kb - 1))
                    >= lax.broadcasted_iota(jnp.int32, (2 * kb, kb), 1), 1.0, 0.0).astype(BF16)
    assert bq & (bq - 1) == 0
    t_loc = lax.broadcasted_iota(jnp.int32, (2 * bq, kb), 0) & (bq - 1)
    causal = lax.broadcasted_iota(jnp.int32, (2 * bq, kb), 1) < t_loc
    return _lane_lo(), tri, causal


def _sb_zero_state(bq):
    return jnp.zeros((2 * bq, 1), F32), jnp.zeros((bq, V7X_LANES), F32)


def _sb_prompt_kernel(q_ref, kt_ref, vt_ref, o_ref):
    i = pl.program_id(2)
    bq = q_ref.shape[1]
    npair = q_ref.shape[2] // V7X_LANES
    lane_lo, tri, causal = _sb_consts(bq, bq)
    row_lo = _row_lo()
    q2 = [_split_heads(q_ref[0, :, _pair(p)], lane_lo) for p in range(npair)]

    def blocks(j):
        cols = pl.ds(pl.multiple_of(j * bq, bq), bq)
        return ([kt_ref[0, _pair(p), cols] for p in range(npair)], [vt_ref[0, _pair(p), cols] for p in range(npair)])

    def sweep(j, states, mask):
        return _sb_blocks(q2, *blocks(j), tri, states, mask, row_lo)

    def sweep2(s, states):
        j = i - 1 - 2 * s
        (ka, va), (kb, vb) = blocks(j), blocks(j - 1)
        z_a = _sb_scores(q2, ka)
        z_b = _sb_scores(q2, kb)
        c_a = _sb_cumsums(z_a, tri, None)
        c_b = _sb_cumsums(z_b, tri, None)
        states = _sb_weigh(z_a, c_a, va, states, None, row_lo)
        return _sb_weigh(z_b, c_b, vb, states, None, row_lo)

    states = sweep(i, (_sb_zero_state(bq),) * npair, causal)
    states = lax.fori_loop(0, i // 2, sweep2, states)
    states = lax.cond(i % 2 == 1, lambda st: sweep(0, st, None), lambda st: st, states)
    for p in range(npair):
        o_ref[0, :, _pair(p)] = states[p][1].astype(BF16)


def _sb_prompt(q, kt, vt):
    b, t, sw = q.shape
    bq = V7X_MXU_DIM
    width = PAIRS_PER_STEP * V7X_LANES
    qspec = pl.BlockSpec((1, bq, width), lambda bi, p, i: (bi, i, p))
    kvspec = pl.BlockSpec((1, width, t), lambda bi, p, i: (bi, p, 0))
    return pl.pallas_call(
        _sb_prompt_kernel,
        grid=(b, sw // width, t // bq),
        in_specs=[qspec, kvspec, kvspec],
        out_specs=qspec,
        out_shape=jax.ShapeDtypeStruct((b, t, sw), BF16),
        compiler_params=_params("arbitrary", "arbitrary", "arbitrary"),
        name="sb_attn_prompt",
    )(q, kt, vt)


def _pair_rows(ref, p):
    return jnp.concatenate([ref[0, 2 * p], ref[0, 2 * p + 1]], axis=0).astype(BF16)


def _sb_sample_kernel(q_ref, kc_ref, vc_ref, kn_ref, vn_ref, o_ref, kpad_ref, vpad_ref):
    bq = q_ref.shape[1]
    kb = kpad_ref.shape[0]
    npair = q_ref.shape[2] // V7X_LANES
    nblk = kc_ref.shape[3] // kb
    lane_lo, tri2, causal = _sb_consts(bq, kb)
    row_lo = _row_lo()
    kpad_ref[...] = jnp.zeros_like(kpad_ref)
    vpad_ref[...] = jnp.zeros_like(vpad_ref)
    kpad_ref[0:bq, :] = kn_ref[0]
    vpad_ref[0:bq, :] = vn_ref[0]
    cols = [slice(j * kb, (j + 1) * kb) for j in reversed(range(nblk))]
    for p in range(npair):
        q2 = _split_heads(q_ref[0, :, _pair(p)], lane_lo)
        kt, vt = _pair_rows(kc_ref, p), _pair_rows(vc_ref, p)
        z = [_dot_t(q2, kpad_ref[:, _pair(p)])] + [_dot(q2, kt[:, c]) for c in cols]
        sp = [_softplus(zb) for zb in z]
        sp[0] = jnp.where(causal, sp[0], 0.0)
        sp_all = jnp.concatenate(sp, axis=0)
        hi = sp_all.astype(BF16)
        lo = (sp_all - hi.astype(F32)).astype(BF16)
        c_all = _dot(jnp.concatenate([hi, lo], axis=1), tri2)
        carry = jnp.zeros((2 * bq, 1), F32)
        ws = []
        for j, zb in enumerate(z):
            c = c_all[j * 2 * bq:(j + 1) * 2 * bq]
            w = jnp.exp2((zb - c - carry) * LOG2E)
            if j == 0:
                w = jnp.where(causal, w, 0.0)
            carry = carry + c[:, 0:1]
            w = w.astype(BF16)
            ws.append(jnp.concatenate([w[:bq], w[bq:]], axis=1))
        vt_cat = jnp.concatenate([_split_heads_t(vt[:, c], row_lo) for c in cols], axis=1)
        acc = _dot(ws[0], _split_heads(vpad_ref[:, _pair(p)], lane_lo)) + _dot_t(jnp.concatenate(ws[1:], axis=1), vt_cat)
        o_ref[0, :, _pair(p)] = acc.astype(BF16)


def _sb_sample(q, kc, vc, kn, vn):
    b, t, sw = q.shape
    new = pl.BlockSpec((1, t, sw), lambda bi: (bi, 0, 0))
    cache = pl.BlockSpec((1,) + kc.shape[1:], lambda bi: (bi, 0, 0, 0))
    return pl.pallas_call(
        _sb_sample_kernel,
        grid=(b,),
        in_specs=[new, cache, cache, new, new],
        out_specs=new,
        out_shape=jax.ShapeDtypeStruct((b, t, sw), BF16),
        scratch_shapes=[pltpu.VMEM((V7X_MXU_DIM, sw), BF16)] * 2,
        compiler_params=_params("arbitrary"),
        name="sb_attn_sample",
    )(q, kc, vc, kn, vn)


def _conv_kernel(u_ref, halo_ref, dw_ref, b_ref, g_ref, be_ref, c_ref, ext_ref, sh_ref, *, halo_is_history):
    tm = u_ref.shape[1]
    halo = halo_ref[0]
    if not halo_is_history:
        halo = jnp.where(pl.program_id(1) > 0, halo, 0.0)
    ext_ref[0:HALO_ROWS, :] = halo
    ext_ref[HALO_ROWS:, :] = u_ref[0]
    n = sh_ref.shape[1]
    for s in range(1, SUBLANES):
        sh_ref[s - 1] = ext_ref[s:s + n, :]
    first = HALO_ROWS - CONV_STATE
    rows = min(CONV_ROWS, tm)
    for r0 in range(0, tm, rows):
        y = b_ref[...]
        for w in range(CONV_WIDTH):
            s = (first + w) % SUBLANES
            a = r0 + first + w - s
            tap = ext_ref[a:a + rows, :] if s == 0 else sh_ref[s - 1, a:a + rows, :]
            y = y + tap * dw_ref[w:w + 1, :]
        mu = jnp.mean(y, axis=-1, keepdims=True)
        var = jnp.mean(jnp.square(y - mu), axis=-1, keepdims=True)
        yn = ((y - mu) * lax.rsqrt(var + LN_EPS)) * g_ref[...] + be_ref[...]
        c_ref[0, r0:r0 + rows, :] = (yn * jax.nn.sigmoid(yn)).astype(BF16)


def _conv(u, halo_src, dw, b, g, be, tm, halo_is_history):
    bsz, t, c = u.shape
    if halo_is_history:
        halo_spec = pl.BlockSpec((1, HALO_ROWS, c), lambda bi, ti: (bi, 0, 0))
    else:
        per = tm // HALO_ROWS
        halo_spec = pl.BlockSpec((1, HALO_ROWS, c), lambda bi, ti: (bi, jnp.maximum(ti * per - 1, 0), 0))
    tile = pl.BlockSpec((1, tm, c), lambda bi, ti: (bi, ti, 0))
    vec = _resident((1, c))
    return pl.pallas_call(
        functools.partial(_conv_kernel, halo_is_history=halo_is_history),
        grid=(bsz, t // tm),
        in_specs=[tile, halo_spec, _resident(dw.shape), vec, vec, vec],
        out_specs=tile,
        out_shape=jax.ShapeDtypeStruct((bsz, t, c), BF16),
        scratch_shapes=[pltpu.VMEM((HALO_ROWS + tm, c), F32),
                        pltpu.VMEM((SUBLANES - 1, HALO_ROWS + tm - SUBLANES, c), F32)],
        compiler_params=_params("arbitrary", "arbitrary"),
        name="conv_module",
    )(u, halo_src, dw, b, g, be)


def _out_mlp_kernel(x_ref, m0_ref, m1_ref, wo_ref, g_ref, wu_ref, wd_ref, gf_ref, y_ref, h_ref, *, final_norm):
    half = m0_ref.shape[-1]
    x1 = x_ref[...] + _dot(m0_ref[...], wo_ref[0:half, :]) + _dot(m1_ref[...], wo_ref[half:, :])
    h_ref[...] = _rms_rows(x1, g_ref[...]).astype(BF16)
    y_ref[...] = x1
    for c0 in range(0, wu_ref.shape[1], FF_CHUNK):
        up = _dot(h_ref[...], wu_ref[:, c0:c0 + FF_CHUNK])
        act = jnp.square(jnp.maximum(up, 0.0)).astype(BF16)
        y_ref[...] += _dot(act, wd_ref[c0:c0 + FF_CHUNK, :])
    if final_norm:
        y_ref[...] = _rms_rows(y_ref[...], gf_ref[...])


def _out_mlp(x, m0, m1, m1_block, wo, g, wu, wd, gf, tm, final_norm):
    m, d = x.shape
    half = wo.shape[0] // 2
    row = pl.BlockSpec((tm, d), lambda i: (i, 0))
    return pl.pallas_call(
        functools.partial(_out_mlp_kernel, final_norm=final_norm),
        grid=(m // tm,),
        in_specs=[row, pl.BlockSpec((tm, half), lambda i: (i, 0)), pl.BlockSpec((tm, half), lambda i: (i, m1_block)),
                  _resident(wo.shape), _resident((1, d)), _resident(wu.shape), _resident(wd.shape), _resident((1, d))],
        out_specs=row,
        out_shape=jax.ShapeDtypeStruct((m, d), F32),
        scratch_shapes=[pltpu.VMEM((tm, d), BF16)],
        compiler_params=_params("arbitrary"),
        name="out_proj_mlp",
    )(x, m0, m1, wo, g, wu, wd, gf)


def _c_in_kernel(x_ref, g_ref, w_ref, q_ref, k_ref, v_ref, kt_ref, vt_ref, h_ref, kf_ref, vf_ref, *, tail_every):
    h_ref[...] = _rms_rows(x_ref[...], g_ref[...]).astype(BF16)
    cw = q_ref.shape[-1]
    half = cw // 2
    for c0 in range(0, cw, half):
        cols = slice(c0, c0 + half)
        q_ref[:, cols] = (_dot(h_ref[...], w_ref[:, c0:c0 + half]) * SCALE).astype(BF16)
        for base, b_ref, f_ref in ((cw, k_ref, kf_ref), (2 * cw, v_ref, vf_ref)):
            f_ref[:, cols] = _dot(h_ref[...], w_ref[:, base + c0:base + c0 + half])
            b_ref[:, cols] = f_ref[:, cols].astype(BF16)

    @pl.when((pl.program_id(0) + 1) % tail_every == 0)
    def _():
        _store_heads(kt_ref, kf_ref)
        _store_heads(vt_ref, vf_ref)


def _c_in(x, g, w, t, keep, tm):
    m, d = x.shape
    cw = w.shape[1] // 3
    nh = cw // HEAD_DIM
    row = lambda n: pl.BlockSpec((tm, n), lambda i: (i, 0))
    if keep == t:
        tail_every = 1
        tail_map = lambda i: (i, 0, 0)
    else:
        assert keep == tm and t % tm == 0
        tail_every = t // tm
        tail_map = lambda i: (i // tail_every, 0, 0)
    tail = pl.BlockSpec((tm, nh, HEAD_DIM), tail_map)
    nb = m // t
    return pl.pallas_call(
        functools.partial(_c_in_kernel, tail_every=tail_every),
        grid=(m // tm,),
        in_specs=[row(d), _resident((1, d)), _resident(w.shape)],
        out_specs=[row(cw), row(cw), row(cw), tail, tail],
        out_shape=[jax.ShapeDtypeStruct((m, cw), BF16)] * 3 + [jax.ShapeDtypeStruct((nb * keep, nh, HEAD_DIM), F32)] * 2,
        scratch_shapes=[pltpu.VMEM((tm, d), BF16)] + [pltpu.VMEM((tm, cw), F32)] * 2,
        compiler_params=_params("arbitrary"),
        name="c_in_proj",
    )(x, g, w)


def _c_in_t_kernel(x_ref, g_ref, w_ref, wkv_ref, q_ref, ktb_ref, vtb_ref, kt_ref, vt_ref, h_ref):
    h_ref[...] = _rms_rows(x_ref[...], g_ref[...]).astype(BF16)
    cw = q_ref.shape[-1]
    half = cw // 2
    for c0 in range(0, cw, half):
        q_ref[:, c0:c0 + half] = (_dot(h_ref[...], w_ref[:, c0:c0 + half]) * SCALE).astype(BF16)
    for base, b_ref, f_ref in ((0, ktb_ref, kt_ref), (cw, vtb_ref, vt_ref)):
        for r0 in range(0, cw, half):
            f_ref[0, r0:r0 + half, :] = _dot_t(wkv_ref[base + r0:base + r0 + half, :], h_ref[...])
            b_ref[0, r0:r0 + half, :] = f_ref[0, r0:r0 + half, :].astype(BF16)


def _c_in_t(x, g, w, wkv_t, t, keep, tm):
    m, d = x.shape
    cw = w.shape[1]
    assert keep == tm and t % tm == 0
    per = t // tm
    nb = m // t
    row = lambda n: pl.BlockSpec((tm, n), lambda i: (i, 0))
    tr = pl.BlockSpec((1, cw, tm), lambda i: (i // per, 0, i % per))
    tail = pl.BlockSpec((1, cw, keep), lambda i: (i // per, 0, 0))
    return pl.pallas_call(
        _c_in_t_kernel,
        grid=(m // tm,),
        in_specs=[row(d), _resident((1, d)), _resident(w.shape), _resident(wkv_t.shape)],
        out_specs=[row(cw), tr, tr, tail, tail],
        out_shape=[jax.ShapeDtypeStruct((m, cw), BF16)] + [jax.ShapeDtypeStruct((nb, cw, t), BF16)] * 2
        + [jax.ShapeDtypeStruct((nb, cw, keep), F32)] * 2,
        scratch_shapes=[pltpu.VMEM((tm, d), BF16)],
        compiler_params=_params("arbitrary"),
        name="c_in_proj_t",
    )(x, g, w, wkv_t)


def _bias_kernel(rb_ref, o_ref):
    rb = rb_ref[...]
    nh = rb.shape[0]
    hi = rb.astype(BF16)
    r1 = rb - hi.astype(F32)
    mid = r1.astype(BF16)
    lo = (r1 - mid.astype(F32)).astype(BF16)
    terms = jnp.concatenate([hi, mid, lo], axis=0)
    ncol, win = rb.shape[1], o_ref.shape[2]
    s = lax.broadcasted_iota(jnp.int32, (1, win), 1)
    kk = lax.broadcasted_iota(jnp.int32, (ncol, win), 0)

    def body(r, carry):
        idx = jnp.clip(BAND_PAST + r - s, -REL_CLIP, REL_CLIP) + REL_CLIP - REL_FIRST
        kc, qc = s >> CHUNK_SHIFT, r >> CHUNK_SHIFT
        visible = (kc >= qc) & (kc <= qc + LEFT_CHUNKS)
        onehot = jnp.where((kk == idx) & visible, 1.0, 0.0).astype(BF16)
        parts = _dot(terms, onehot)
        row = (parts[0:nh] + parts[nh:2 * nh]) + parts[2 * nh:3 * nh]
        o_ref[r] = jnp.where(visible, row, -jnp.inf)
        return carry

    lax.fori_loop(0, o_ref.shape[0], body, 0, unroll=BIAS_UNROLL)


def _bias_tile(rel_bias):
    h = rel_bias.shape[0]
    cols = rel_bias[:, REL_FIRST:]
    cols = jnp.pad(cols, ((0, 0), (0, REL_COLS - cols.shape[1])))
    tile = pl.pallas_call(
        _bias_kernel,
        out_shape=jax.ShapeDtypeStruct((BAND_Q, h, BAND_WIN), F32),
        compiler_params=pltpu.CompilerParams(vmem_limit_bytes=V7X_VMEM_LIMIT),
        name="band_bias_tile",
    )(cols)
    return jnp.transpose(tile, (1, 0, 2))


def _band_core(q_pairs, kwts, vwts, biases, valid, lane_lo, row_lo):
    bq = q_pairs[0].shape[0]
    zs = [_dot(_split_heads(q_pair, lane_lo), kwt) for q_pair, kwt in zip(q_pairs, kwts)]
    es, invs = [], []
    for z, (bias0, bias1), ok in zip(zs, biases, valid):
        s = z + jnp.concatenate([bias0, bias1], axis=0)
        s = jnp.where(ok, s, -jnp.inf)
        e = jnp.exp2((s - jnp.max(s, axis=-1, keepdims=True)) * LOG2E)
        invs.append(1.0 / jnp.sum(e, axis=-1, keepdims=True))
        es.append(e.astype(BF16))
    outs = []
    for e, inv, vwt in zip(es, invs, vwts):
        o = _dot_t(jnp.concatenate([e[:bq], e[bq:]], axis=1), _split_heads_t(vwt, row_lo))
        outs.append((o * jnp.where(lane_lo, inv[:bq], inv[bq:])).astype(BF16))
    return outs


def _band_prompt_kernel(q_ref, kt_ref, vt_ref, bias_ref, o_ref, kpad_ref, vpad_ref):
    i = pl.program_id(2)
    bq, win = bias_ref.shape[1], bias_ref.shape[2]
    nblk = q_ref.shape[1] // bq
    npair = q_ref.shape[2] // V7X_LANES
    past = win - bq

    @pl.when(i == 0)
    def _():
        kpad_ref[:, 0:past] = jnp.zeros((kpad_ref.shape[0], past), BF16)
        vpad_ref[:, 0:past] = jnp.zeros((vpad_ref.shape[0], past), BF16)
        kpad_ref[:, past:] = kt_ref[0]
        vpad_ref[:, past:] = vt_ref[0]

    col = lax.broadcasted_iota(jnp.int32, (1, win), 1)
    work = [(j, p) for j in range(nblk) for p in range(npair)]
    cols = [pl.ds(pl.multiple_of((i * nblk + j) * bq, bq), win) for j in range(nblk)]
    valid = [col >= past - (i * nblk + j) * bq for j in range(nblk)]
    outs = _band_core([q_ref[0, j * bq:(j + 1) * bq, _pair(p)] for j, p in work],
                      [kpad_ref[_pair(p), cols[j]] for j, p in work], [vpad_ref[_pair(p), cols[j]] for j, p in work],
                      [(bias_ref[2 * p], bias_ref[2 * p + 1]) for j, p in work], [valid[j] for j, p in work],
                      _lane_lo(), _row_lo())
    for (j, p), out in zip(work, outs):
        o_ref[0, j * bq:(j + 1) * bq, _pair(p)] = out


def _band_prompt(q, kt, vt, bias):
    b, t, cw = q.shape
    bq, win = bias.shape[1], bias.shape[2]
    width = PAIRS_PER_STEP * V7X_LANES
    rows = BAND_BLOCKS_PER_STEP * bq
    qspec = pl.BlockSpec((1, rows, width), lambda p, bi, i: (bi, i, p))
    kvspec = pl.BlockSpec((1, width, t), lambda p, bi, i: (bi, p, 0))
    return pl.pallas_call(
        _band_prompt_kernel,
        grid=(cw // width, b, t // rows),
        in_specs=[qspec, kvspec, kvspec, pl.BlockSpec((2 * PAIRS_PER_STEP, bq, win), lambda p, bi, i: (p, 0, 0))],
        out_specs=qspec,
        out_shape=jax.ShapeDtypeStruct((b, t, cw), BF16),
        scratch_shapes=[pltpu.VMEM((width, win - bq + t), BF16)] * 2,
        compiler_params=_params("arbitrary", "arbitrary", "arbitrary"),
        name="band_attn_prompt",
    )(q, kt, vt, bias)


def _band_sample_kernel(q_ref, kc_ref, vc_ref, kn_ref, vn_ref, bias_ref, o_ref, kpad_ref, vpad_ref):
    t = q_ref.shape[1]
    npair = q_ref.shape[2] // V7X_LANES
    past = kc_ref.shape[3]
    win = bias_ref.shape[2]
    kpad_ref[...] = jnp.zeros_like(kpad_ref)
    vpad_ref[...] = jnp.zeros_like(vpad_ref)
    kpad_ref[0:t, :] = kn_ref[0]
    vpad_ref[0:t, :] = vn_ref[0]
    valid = lax.broadcasted_iota(jnp.int32, (1, win), 1) < past + t
    lane_lo, row_lo = _lane_lo(), _row_lo()
    pairs = range(npair)
    q2s = [_split_heads(q_ref[0, :, _pair(p)], lane_lo) for p in pairs]
    zs = [jnp.concatenate([_dot(q2s[p], _pair_rows(kc_ref, p)), _dot_t(q2s[p], kpad_ref[:, _pair(p)])], axis=1)
          for p in pairs]
    for p in pairs:
        s = zs[p] + jnp.concatenate([bias_ref[2 * p], bias_ref[2 * p + 1]], axis=0)
        s = jnp.where(valid, s, -jnp.inf)
        e = jnp.exp2((s - jnp.max(s, axis=-1, keepdims=True)) * LOG2E)
        inv = 1.0 / jnp.sum(e, axis=-1, keepdims=True)
        e = e.astype(BF16)
        e_cache = jnp.concatenate([e[:t, :past], e[t:, :past]], axis=1)
        e_new = jnp.concatenate([e[:t, past:], e[t:, past:]], axis=1)
        o = (_dot_t(e_cache, _split_heads_t(_pair_rows(vc_ref, p), row_lo))
             + _dot(e_new, _split_heads(vpad_ref[:, _pair(p)], lane_lo)))
        o_ref[0, :, _pair(p)] = (o * jnp.where(lane_lo, inv[:t], inv[t:])).astype(BF16)


def _band_sample(q, kc, vc, kn, vn, bias):
    b, t, cw = q.shape
    nh = bias.shape[0]
    past, win = kc.shape[3], bias.shape[2]
    new = pl.BlockSpec((1, t, cw), lambda bi: (bi, 0, 0))
    cache = pl.BlockSpec((1,) + kc.shape[1:], lambda bi: (bi, 0, 0, 0))
    return pl.pallas_call(
        _band_sample_kernel,
        grid=(b,),
        in_specs=[new, cache, cache, new, new, pl.BlockSpec((nh, t, win), lambda bi: (0, 0, 0))],
        out_specs=new,
        out_shape=jax.ShapeDtypeStruct((b, t, cw), BF16),
        scratch_shapes=[pltpu.VMEM((win - past, cw), BF16)] * 2,
        compiler_params=_params("arbitrary"),
        name="band_attn_sample",
    )(q, kc, vc, kn, vn, bias)


def kernel(x_prompt, x_sample, cache_sb_k, cache_sb_v, cache_conv, cache_band_k, cache_band_v, norm_mix, norm_ffn,
           norm_final, w_in_ab, w_out_ab, dw_w, dw_b, conv_ln_g, conv_ln_b, w_in_c, w_out_c, rel_bias, w_up, w_down):
    b, t, d = x_prompt.shape
    bs, ts, _ = x_sample.shape
    n_sb, n_c = cache_sb_k.shape[3], cache_band_k.shape[3]
    sw, cc, cw = n_sb * HEAD_DIM, dw_w.shape[2], n_c * HEAD_DIM
    past = cache_sb_k.shape[2]
    band_past = cache_band_k.shape[2]
    keep = min(BAND_PAST, t)
    assert w_in_ab.shape[0] == 1 and w_in_c.shape[0] == 1 and norm_mix.shape[0] == 2
    assert sw == cc and sw + cc == d and cw == d and w_in_ab.shape[2] == 3 * sw + 2 * cc
    assert t % ROW_TILE == 0 and ts % 16 == 0 and ts <= V7X_MXU_DIM
    assert past % V7X_MXU_DIM == 0 and band_past == BAND_PAST and band_past + ts <= BAND_WIN

    tm_p, tm_s = ROW_TILE, bs * ts
    vec = lambda a: a.reshape(1, -1)
    w_in0, w_out0 = w_in_ab[0].astype(BF16), w_out_ab[0].astype(BF16)
    w_in1, w_out1 = w_in_c[0].astype(BF16), w_out_c[0].astype(BF16)
    w_up_b, w_down_b = w_up.astype(BF16), w_down.astype(BF16)
    dw_pad = jnp.pad(dw_w[0], ((0, HALO_ROWS - CONV_WIDTH), (0, 0)))
    conv_vecs = (vec(dw_b[0]), vec(conv_ln_g[0]), vec(conv_ln_b[0]))
    bias = _bias_tile(rel_bias[0])

    shp = lambda z, n: z.reshape(b, t, n)
    x0 = x_prompt.reshape(b * t, d)
    wkv0_t = jnp.transpose(w_in0[:, sw:3 * sw])
    q, ktb, vtb, kp, vp, up = _ab_in_t(x0, vec(norm_mix[0]), w_in0, wkv0_t, sw, cc, t, tm_p)
    a = _sb_prompt(shp(q, sw), ktb, vtb).reshape(b * t, sw)
    c = _conv(shp(up, cc), shp(up, cc), dw_pad, *conv_vecs, CONV_TILE, False).reshape(b * t, cc)
    xp2 = _out_mlp(x0, a, c, 0, w_out0, vec(norm_ffn[0]), w_up_b[0], w_down_b[0], vec(norm_final), tm_p, False)
    wkv1_t = jnp.transpose(w_in1[:, cw:])
    q, ktb, vtb, bkp, bvp = _c_in_t(xp2, vec(norm_mix[1]), w_in1[:, :cw], wkv1_t, t, keep, tm_p)
    o = _band_prompt(shp(q, cw), ktb, vtb, bias).reshape(b * t, cw)
    yp = _out_mlp(xp2, o, o, 1, w_out1, vec(norm_ffn[1]), w_up_b[1], w_down_b[1], vec(norm_final), tm_p, True)
    by_time = lambda zt, nh: jnp.transpose(zt.reshape(b, nh, HEAD_DIM, -1), (0, 3, 1, 2))[None]

    def layer0(x, tm, attn, conv):
        q, kb, vb, k, v, u = _ab_in(x, vec(norm_mix[0]), w_in0, sw, cc, tm)
        a = attn(q, kb, vb)
        x2 = _out_mlp(x, a, conv(u), 0, w_out0, vec(norm_ffn[0]), w_up_b[0], w_down_b[0], vec(norm_final), tm, False)
        return x2, k, v, u

    def layer1(x, tm, tlen, tail, attn):
        q, k, v, kt, vt = _c_in(x, vec(norm_mix[1]), w_in1, tlen, tail, tm)
        o = attn(q, k, v)
        y = _out_mlp(x, o, o, 1, w_out1, vec(norm_ffn[1]), w_up_b[1], w_down_b[1], vec(norm_final), tm, True)
        return y, kt, vt

    shs = lambda z, n: z.reshape(bs, ts, n)
    conv_hist = jnp.pad(cache_conv[0], ((0, 0), (HALO_ROWS - CONV_STATE, 0), (0, 0)))
    by_head = lambda c: jnp.transpose(c[0], (0, 2, 3, 1))
    ck, cv = by_head(cache_sb_k), by_head(cache_sb_v)
    xs2, ks, vs, us = layer0(
        x_sample.reshape(bs * ts, d), tm_s,
        lambda q, k, v: _sb_sample(shs(q, sw), ck, cv, shs(k, sw), shs(v, sw)).reshape(bs * ts, sw),
        lambda u: _conv(shs(u, cc), conv_hist, dw_pad, *conv_vecs, ts, True).reshape(bs * ts, cc))
    bck, bcv = by_head(cache_band_k), by_head(cache_band_v)
    ys, bks, bvs = layer1(
        xs2, tm_s, ts, ts,
        lambda q, k, v: _band_sample(shs(q, cw), bck, bcv, shs(k, cw), shs(v, cw), bias).reshape(bs * ts, cw))

    new_conv_p = up.reshape(b, t, cc)[:, t - CONV_STATE:]
    new_conv_s = jnp.concatenate([cache_conv[0], us.reshape(bs, ts, cc)], axis=1)[:, ts:]
    return (yp.reshape(b, t, d), ys.reshape(bs, ts, d),
            by_time(kp, n_sb), by_time(vp, n_sb), new_conv_p[None], by_time(bkp, n_c), by_time(bvp, n_c),
            ks.reshape(1, bs, ts, n_sb, HEAD_DIM), vs.reshape(1, bs, ts, n_sb, HEAD_DIM), new_conv_s[None],
            bks.reshape(1, bs, ts, n_c, HEAD_DIM), bvs.reshape(1, bs, ts, n_c, HEAD_DIM))
```

```python
import functools

import jax
import jax.numpy as jnp
from jax import lax
from jax.experimental import pallas as pl
from jax.experimental.pallas import tpu as pltpu

F32 = jnp.float32
BF16 = jnp.bfloat16

HEAD_DIM = 64
CHUNK = 64
CHUNK_SHIFT = 6
LEFT_CHUNKS = 8
BAND_PAST = LEFT_CHUNKS * CHUNK
REL_CLIP = 128
CONV_WIDTH = 31
CONV_STATE = CONV_WIDTH - 1
RMS_EPS = 1e-6
LN_EPS = 1e-5
LOG2E = 1.4426950408889634
SCALE = HEAD_DIM ** -0.5

V7X_LANES = 128
SUBLANES = 8
V7X_MXU_DIM = 256
V7X_VMEM_LIMIT = 56 * 1024 * 1024
ROW_TILE = 512
FF_CHUNK = 512
CONV_TILE = 256
HALO_ROWS = 32
CONV_ROWS = 64
PAIRS_PER_STEP = 4
BAND_Q = 256
BAND_WIN = BAND_PAST + BAND_Q
BAND_BLOCKS_PER_STEP = 2
REL_FIRST = REL_CLIP + 1 - CHUNK
REL_COLS = 256
BIAS_UNROLL = 8


def _params(*semantics):
    return pltpu.CompilerParams(dimension_semantics=semantics, vmem_limit_bytes=V7X_VMEM_LIMIT)


def _dot(a, b):
    return jnp.dot(a, b, preferred_element_type=F32)


def _dot_t(a, b):
    return lax.dot_general(a, b, (((1,), (1,)), ((), ())), preferred_element_type=F32)


def _rms_rows(x, g):
    r = lax.rsqrt(jnp.mean(x * x, axis=-1, keepdims=True) + RMS_EPS)
    return (x * r) * g


def _resident(shape):
    return pl.BlockSpec(shape, lambda *_: (0,) * len(shape), pipeline_mode=pl.Buffered(1))


def _lane_lo():
    return lax.broadcasted_iota(jnp.int32, (1, V7X_LANES), 1) < HEAD_DIM


def _row_lo():
    return lax.broadcasted_iota(jnp.int32, (V7X_LANES, 1), 0) < HEAD_DIM


def _split_heads(xp, lane_lo):
    zero = jnp.zeros_like(xp)
    return jnp.concatenate([jnp.where(lane_lo, xp, zero), jnp.where(lane_lo, zero, xp)], axis=0)


def _split_heads_t(xt, row_lo):
    zero = jnp.zeros_like(xt)
    return jnp.concatenate([jnp.where(row_lo, xt, zero), jnp.where(row_lo, zero, xt)], axis=1)


def _pair(p):
    return slice(p * V7X_LANES, (p + 1) * V7X_LANES)


def _store_heads(dst_ref, src_ref):
    nh = dst_ref.shape[1]
    per_head = jnp.stack([src_ref[:, h * HEAD_DIM:(h + 1) * HEAD_DIM] for h in range(nh)], axis=0)
    dst_ref[...] = jnp.swapaxes(per_head, 0, 1)


def _ab_in_kernel(x_ref, g_ref, w_ref, q_ref, kb_ref, vb_ref, k_ref, v_ref, u_ref, h_ref, kf_ref, vf_ref):
    h_ref[...] = _rms_rows(x_ref[...], g_ref[...]).astype(BF16)
    sw = q_ref.shape[-1]
    cw = u_ref.shape[-1]
    proj = lambda c0, n: _dot(h_ref[...], w_ref[:, c0:c0 + n])
    q_ref[...] = (proj(0, sw) * SCALE).astype(BF16)
    for c0, b_ref, o_ref, f_ref in ((sw, kb_ref, k_ref, kf_ref), (2 * sw, vb_ref, v_ref, vf_ref)):
        f_ref[...] = proj(c0, sw)
        b_ref[...] = f_ref[...].astype(BF16)
        _store_heads(o_ref, f_ref)
    u_ref[...] = proj(3 * sw, cw) * jax.nn.sigmoid(proj(3 * sw + cw, cw))


def _ab_in(x, g, w, sw, cw, tm):
    m, d = x.shape
    nh = sw // HEAD_DIM
    row = lambda n: pl.BlockSpec((tm, n), lambda i: (i, 0))
    heads = pl.BlockSpec((tm, nh, HEAD_DIM), lambda i: (i, 0, 0))
    return pl.pallas_call(
        _ab_in_kernel,
        grid=(m // tm,),
        in_specs=[row(d), _resident((1, d)), _resident(w.shape)],
        out_specs=[row(sw)] * 3 + [heads] * 2 + [row(cw)],
        out_shape=[jax.ShapeDtypeStruct((m, sw), BF16)] * 3 + [jax.ShapeDtypeStruct((m, nh, HEAD_DIM), F32)] * 2
        + [jax.ShapeDtypeStruct((m, cw), F32)],
        scratch_shapes=[pltpu.VMEM((tm, d), BF16)] + [pltpu.VMEM((tm, sw), F32)] * 2,
        compiler_params=_params("arbitrary"),
        name="ab_in_proj",
    )(x, g, w)


def _ab_in_t_kernel(x_ref, g_ref, w_ref, wkv_ref, q_ref, ktb_ref, vtb_ref, kt_ref, vt_ref, u_ref, h_ref):
    h_ref[...] = _rms_rows(x_ref[...], g_ref[...]).astype(BF16)
    sw = q_ref.shape[-1]
    cw = u_ref.shape[-1]
    proj = lambda c0, n: _dot(h_ref[...], w_ref[:, c0:c0 + n])
    q_ref[...] = (proj(0, sw) * SCALE).astype(BF16)
    for r0, b_ref, f_ref in ((0, ktb_ref, kt_ref), (sw, vtb_ref, vt_ref)):
        f_ref[0] = _dot_t(wkv_ref[r0:r0 + sw, :], h_ref[...])
        b_ref[0] = f_ref[0].astype(BF16)
    u_ref[...] = proj(3 * sw, cw) * jax.nn.sigmoid(proj(3 * sw + cw, cw))


def _ab_in_t(x, g, w, wkv_t, sw, cw, t, tm):
    m, d = x.shape
    per = t // tm
    row = lambda n: pl.BlockSpec((tm, n), lambda i: (i, 0))
    tr = pl.BlockSpec((1, sw, tm), lambda i: (i // per, 0, i % per))
    tshape = lambda dt: jax.ShapeDtypeStruct((m // t, sw, t), dt)
    return pl.pallas_call(
        _ab_in_t_kernel,
        grid=(m // tm,),
        in_specs=[row(d), _resident((1, d)), _resident(w.shape), _resident(wkv_t.shape)],
        out_specs=[row(sw), tr, tr, tr, tr, row(cw)],
        out_shape=[jax.ShapeDtypeStruct((m, sw), BF16), tshape(BF16), tshape(BF16), tshape(F32), tshape(F32),
                   jax.ShapeDtypeStruct((m, cw), F32)],
        scratch_shapes=[pltpu.VMEM((tm, d), BF16)],
        compiler_params=_params("arbitrary"),
        name="ab_in_proj_t",
    )(x, g, w, wkv_t)


def _softplus(z):
    return jnp.maximum(z, 0.0) + jnp.log(1.0 + jnp.exp2(jnp.abs(z) * -LOG2E))


def _sb_scores(q2s, ktblks):
    return [_dot(q2, ktblk) for q2, ktblk in zip(q2s, ktblks)]


def _sb_cumsums(zs, tri2, mask):
    cs = []
    for z in zs:
        sp = _softplus(z)
        if mask is not None:
            sp = jnp.where(mask, sp, 0.0)
        hi = sp.astype(BF16)
        lo = (sp - hi.astype(F32)).astype(BF16)
        cs.append(_dot(jnp.concatenate([hi, lo], axis=1), tri2))
    return cs


def _sb_weigh(zs, cs, vtblks, states, mask, row_lo):
    out = []
    for z, c, vtblk, (carry, acc) in zip(zs, cs, vtblks, states):
        bq = acc.shape[0]
        w = jnp.exp2((z - c - carry) * LOG2E)
        if mask is not None:
            w = jnp.where(mask, w, 0.0)
        w = w.astype(BF16)
        wcat = jnp.concatenate([w[:bq], w[bq:]], axis=1)
        out.append((carry + c[:, 0:1], acc + _dot_t(wcat, _split_heads_t(vtblk, row_lo))))
    return tuple(out)


def _sb_blocks(q2s, ktblks, vtblks, tri2, states, mask, row_lo):
    zs = _sb_scores(q2s, ktblks)
    return _sb_weigh(zs, _sb_cumsums(zs, tri2, mask), vtblks, states, mask, row_lo)


def _sb_consts(bq, kb):
    tri = jnp.where((lax.broadcasted_iota(jnp.int32, (2 * kb, kb), 0) & (kb - 1))
                    >= lax.broadcasted_iota(jnp.int32, (2 * kb, kb), 1), 1.0, 0.0).astype(BF16)
    assert bq & (bq - 1) == 0
    t_loc = lax.broadcasted_iota(jnp.int32, (2 * bq, kb), 0) & (bq - 1)
    causal = lax.broadcasted_iota(jnp.int32, (2 * bq, kb), 1) < t_loc
    return _lane_lo(), tri, causal


def _sb_zero_state(bq):
    return jnp.zeros((2 * bq, 1), F32), jnp.zeros((bq, V7X_LANES), F32)


def _sb_prompt_kernel(q_ref, kt_ref, vt_ref, o_ref):
    i = pl.program_id(2)
    bq = q_ref.shape[1]
    npair = q_ref.shape[2] // V7X_LANES
    lane_lo, tri, causal = _sb_consts(bq, bq)
    row_lo = _row_lo()
    q2 = [_split_heads(q_ref[0, :, _pair(p)], lane_lo) for p in range(npair)]

    def blocks(j):
        cols = pl.ds(pl.multiple_of(j * bq, bq), bq)
        return ([kt_ref[0, _pair(p), cols] for p in range(npair)], [vt_ref[0, _pair(p), cols] for p in range(npair)])

    def sweep(j, states, mask):
        return _sb_blocks(q2, *blocks(j), tri, states, mask, row_lo)

    def sweep_two(ja, jb, states, mask_a):
        (ka, va), (kb, vb) = blocks(ja), blocks(jb)
        z_a = _sb_scores(q2, ka)
        z_b = _sb_scores(q2, kb)
        c_a = _sb_cumsums(z_a, tri, mask_a)
        c_b = _sb_cumsums(z_b, tri, None)
        states = _sb_weigh(z_a, c_a, va, states, mask_a, row_lo)
        return _sb_weigh(z_b, c_b, vb, states, None, row_lo)

    zero = (_sb_zero_state(bq),) * npair
    states = lax.cond(i >= 1, lambda st: sweep_two(i, i - 1, st, causal), lambda st: sweep(i, st, causal), zero)
    rest = jnp.maximum(i - 1, 0)
    states = lax.fori_loop(0, rest // 2, lambda s, st: sweep_two(i - 2 - 2 * s, i - 3 - 2 * s, st, None), states)
    states = lax.cond(rest % 2 == 1, lambda st: sweep(0, st, None), lambda st: st, states)
    for p in range(npair):
        o_ref[0, :, _pair(p)] = states[p][1].astype(BF16)


def _sb_prompt(q, kt, vt):
    b, t, sw = q.shape
    bq = V7X_MXU_DIM
    width = PAIRS_PER_STEP * V7X_LANES
    qspec = pl.BlockSpec((1, bq, width), lambda bi, p, i: (bi, i, p))
    kvspec = pl.BlockSpec((1, width, t), lambda bi, p, i: (bi, p, 0))
    return pl.pallas_call(
        _sb_prompt_kernel,
        grid=(b, sw // width, t // bq),
        in_specs=[qspec, kvspec, kvspec],
        out_specs=qspec,
        out_shape=jax.ShapeDtypeStruct((b, t, sw), BF16),
        compiler_params=_params("arbitrary", "arbitrary", "arbitrary"),
        name="sb_attn_prompt",
    )(q, kt, vt)


def _pair_rows(ref, p):
    return jnp.concatenate([ref[0, 2 * p], ref[0, 2 * p + 1]], axis=0).astype(BF16)


def _sb_sample_kernel(q_ref, kc_ref, vc_ref, kn_ref, vn_ref, o_ref, kpad_ref, vpad_ref):
    bq = q_ref.shape[1]
    kb = kpad_ref.shape[0]
    npair = q_ref.shape[2] // V7X_LANES
    nblk = kc_ref.shape[3] // kb
    lane_lo, tri2, causal = _sb_consts(bq, kb)
    row_lo = _row_lo()
    kpad_ref[...] = jnp.zeros_like(kpad_ref)
    vpad_ref[...] = jnp.zeros_like(vpad_ref)
    kpad_ref[0:bq, :] = kn_ref[0]
    vpad_ref[0:bq, :] = vn_ref[0]
    cols = [slice(j * kb, (j + 1) * kb) for j in reversed(range(nblk))]
    for p in range(npair):
        q2 = _split_heads(q_ref[0, :, _pair(p)], lane_lo)
        kt, vt = _pair_rows(kc_ref, p), _pair_rows(vc_ref, p)
        z = [_dot_t(q2, kpad_ref[:, _pair(p)])] + [_dot(q2, kt[:, c]) for c in cols]
        sp = [_softplus(zb) for zb in z]
        sp[0] = jnp.where(causal, sp[0], 0.0)
        sp_all = jnp.concatenate(sp, axis=0)
        hi = sp_all.astype(BF16)
        lo = (sp_all - hi.astype(F32)).astype(BF16)
        c_all = _dot(jnp.concatenate([hi, lo], axis=1), tri2)
        carry = jnp.zeros((2 * bq, 1), F32)
        ws = []
        for j, zb in enumerate(z):
            c = c_all[j * 2 * bq:(j + 1) * 2 * bq]
            w = jnp.exp2((zb - c - carry) * LOG2E)
            if j == 0:
                w = jnp.where(causal, w, 0.0)
            carry = carry + c[:, 0:1]
            w = w.astype(BF16)
            ws.append(jnp.concatenate([w[:bq], w[bq:]], axis=1))
        vt_cat = jnp.concatenate([_split_heads_t(vt[:, c], row_lo) for c in cols], axis=1)
        acc = _dot(ws[0], _split_heads(vpad_ref[:, _pair(p)], lane_lo)) + _dot_t(jnp.concatenate(ws[1:], axis=1), vt_cat)
        o_ref[0, :, _pair(p)] = acc.astype(BF16)


def _sb_sample(q, kc, vc, kn, vn):
    b, t, sw = q.shape
    new = pl.BlockSpec((1, t, sw), lambda bi: (bi, 0, 0))
    cache = pl.BlockSpec((1,) + kc.shape[1:], lambda bi: (bi, 0, 0, 0))
    return pl.pallas_call(
        _sb_sample_kernel,
        grid=(b,),
        in_specs=[new, cache, cache, new, new],
        out_specs=new,
        out_shape=jax.ShapeDtypeStruct((b, t, sw), BF16),
        scratch_shapes=[pltpu.VMEM((V7X_MXU_DIM, sw), BF16)] * 2,
        compiler_params=_params("arbitrary"),
        name="sb_attn_sample",
    )(q, kc, vc, kn, vn)


def _conv_kernel(u_ref, halo_ref, dw_ref, b_ref, g_ref, be_ref, c_ref, ext_ref, sh_ref, *, halo_is_history):
    tm = u_ref.shape[1]
    halo = halo_ref[0]
    if not halo_is_history:
        halo = jnp.where(pl.program_id(1) > 0, halo, 0.0)
    ext_ref[0:HALO_ROWS, :] = halo
    ext_ref[HALO_ROWS:, :] = u_ref[0]
    n = sh_ref.shape[1]
    for s in range(1, SUBLANES):
        sh_ref[s - 1] = ext_ref[s:s + n, :]
    first = HALO_ROWS - CONV_STATE
    rows = min(CONV_ROWS, tm)
    for r0 in range(0, tm, rows):
        y = b_ref[...]
        for w in range(CONV_WIDTH):
            s = (first + w) % SUBLANES
            a = r0 + first + w - s
            tap = ext_ref[a:a + rows, :] if s == 0 else sh_ref[s - 1, a:a + rows, :]
            y = y + tap * dw_ref[w:w + 1, :]
        mu = jnp.mean(y, axis=-1, keepdims=True)
        var = jnp.mean(jnp.square(y - mu), axis=-1, keepdims=True)
        yn = ((y - mu) * lax.rsqrt(var + LN_EPS)) * g_ref[...] + be_ref[...]
        c_ref[0, r0:r0 + rows, :] = (yn * jax.nn.sigmoid(yn)).astype(BF16)


def _conv(u, halo_src, dw, b, g, be, tm, halo_is_history):
    bsz, t, c = u.shape
    if halo_is_history:
        halo_spec = pl.BlockSpec((1, HALO_ROWS, c), lambda bi, ti: (bi, 0, 0))
    else:
        per = tm // HALO_ROWS
        halo_spec = pl.BlockSpec((1, HALO_ROWS, c), lambda bi, ti: (bi, jnp.maximum(ti * per - 1, 0), 0))
    tile = pl.BlockSpec((1, tm, c), lambda bi, ti: (bi, ti, 0))
    vec = _resident((1, c))
    return pl.pallas_call(
        functools.partial(_conv_kernel, halo_is_history=halo_is_history),
        grid=(bsz, t // tm),
        in_specs=[tile, halo_spec, _resident(dw.shape), vec, vec, vec],
        out_specs=tile,
        out_shape=jax.ShapeDtypeStruct((bsz, t, c), BF16),
        scratch_shapes=[pltpu.VMEM((HALO_ROWS + tm, c), F32),
                        pltpu.VMEM((SUBLANES - 1, HALO_ROWS + tm - SUBLANES, c), F32)],
        compiler_params=_params("arbitrary", "arbitrary"),
        name="conv_module",
    )(u, halo_src, dw, b, g, be)


def _out_mlp_kernel(x_ref, m0_ref, m1_ref, wo_ref, g_ref, wu_ref, wd_ref, gf_ref, y_ref, h_ref, *, final_norm):
    half = m0_ref.shape[-1]
    x1 = x_ref[...] + _dot(m0_ref[...], wo_ref[0:half, :]) + _dot(m1_ref[...], wo_ref[half:, :])
    h_ref[...] = _rms_rows(x1, g_ref[...]).astype(BF16)
    y_ref[...] = x1
    for c0 in range(0, wu_ref.shape[1], FF_CHUNK):
        up = _dot(h_ref[...], wu_ref[:, c0:c0 + FF_CHUNK])
        act = jnp.square(jnp.maximum(up, 0.0)).astype(BF16)
        y_ref[...] += _dot(act, wd_ref[c0:c0 + FF_CHUNK, :])
    if final_norm:
        y_ref[...] = _rms_rows(y_ref[...], gf_ref[...])


def _out_mlp(x, m0, m1, m1_block, wo, g, wu, wd, gf, tm, final_norm):
    m, d = x.shape
    half = wo.shape[0] // 2
    row = pl.BlockSpec((tm, d), lambda i: (i, 0))
    return pl.pallas_call(
        functools.partial(_out_mlp_kernel, final_norm=final_norm),
        grid=(m // tm,),
        in_specs=[row, pl.BlockSpec((tm, half), lambda i: (i, 0)), pl.BlockSpec((tm, half), lambda i: (i, m1_block)),
                  _resident(wo.shape), _resident((1, d)), _resident(wu.shape), _resident(wd.shape), _resident((1, d))],
        out_specs=row,
        out_shape=jax.ShapeDtypeStruct((m, d), F32),
        scratch_shapes=[pltpu.VMEM((tm, d), BF16)],
        compiler_params=_params("arbitrary"),
        name="out_proj_mlp",
    )(x, m0, m1, wo, g, wu, wd, gf)


def _c_in_kernel(x_ref, g_ref, w_ref, q_ref, k_ref, v_ref, kt_ref, vt_ref, h_ref, kf_ref, vf_ref, *, tail_every):
    h_ref[...] = _rms_rows(x_ref[...], g_ref[...]).astype(BF16)
    cw = q_ref.shape[-1]
    half = cw // 2
    for c0 in range(0, cw, half):
        cols = slice(c0, c0 + half)
        q_ref[:, cols] = (_dot(h_ref[...], w_ref[:, c0:c0 + half]) * SCALE).astype(BF16)
        for base, b_ref, f_ref in ((cw, k_ref, kf_ref), (2 * cw, v_ref, vf_ref)):
            f_ref[:, cols] = _dot(h_ref[...], w_ref[:, base + c0:base + c0 + half])
            b_ref[:, cols] = f_ref[:, cols].astype(BF16)

    @pl.when((pl.program_id(0) + 1) % tail_every == 0)
    def _():
        _store_heads(kt_ref, kf_ref)
        _store_heads(vt_ref, vf_ref)


def _c_in(x, g, w, t, keep, tm):
    m, d = x.shape
    cw = w.shape[1] // 3
    nh = cw // HEAD_DIM
    row = lambda n: pl.BlockSpec((tm, n), lambda i: (i, 0))
    if keep == t:
        tail_every = 1
        tail_map = lambda i: (i, 0, 0)
    else:
        assert keep == tm and t % tm == 0
        tail_every = t // tm
        tail_map = lambda i: (i // tail_every, 0, 0)
    tail = pl.BlockSpec((tm, nh, HEAD_DIM), tail_map)
    nb = m // t
    return pl.pallas_call(
        functools.partial(_c_in_kernel, tail_every=tail_every),
        grid=(m // tm,),
        in_specs=[row(d), _resident((1, d)), _resident(w.shape)],
        out_specs=[row(cw), row(cw), row(cw), tail, tail],
        out_shape=[jax.ShapeDtypeStruct((m, cw), BF16)] * 3 + [jax.ShapeDtypeStruct((nb * keep, nh, HEAD_DIM), F32)] * 2,
        scratch_shapes=[pltpu.VMEM((tm, d), BF16)] + [pltpu.VMEM((tm, cw), F32)] * 2,
        compiler_params=_params("arbitrary"),
        name="c_in_proj",
    )(x, g, w)


def _c_in_t_kernel(x_ref, g_ref, w_ref, wkv_ref, q_ref, ktb_ref, vtb_ref, kt_ref, vt_ref, h_ref):
    h_ref[...] = _rms_rows(x_ref[...], g_ref[...]).astype(BF16)
    cw = q_ref.shape[-1]
    half = cw // 2
    for c0 in range(0, cw, half):
        q_ref[:, c0:c0 + half] = (_dot(h_ref[...], w_ref[:, c0:c0 + half]) * SCALE).astype(BF16)
    for base, b_ref, f_ref in ((0, ktb_ref, kt_ref), (cw, vtb_ref, vt_ref)):
        for r0 in range(0, cw, half):
            f_ref[0, r0:r0 + half, :] = _dot_t(wkv_ref[base + r0:base + r0 + half, :], h_ref[...])
            b_ref[0, r0:r0 + half, :] = f_ref[0, r0:r0 + half, :].astype(BF16)


def _c_in_t(x, g, w, wkv_t, t, keep, tm):
    m, d = x.shape
    cw = w.shape[1]
    assert keep == tm and t % tm == 0
    per = t // tm
    nb = m // t
    row = lambda n: pl.BlockSpec((tm, n), lambda i: (i, 0))
    tr = pl.BlockSpec((1, cw, tm), lambda i: (i // per, 0, i % per))
    tail = pl.BlockSpec((1, cw, keep), lambda i: (i // per, 0, 0))
    return pl.pallas_call(
        _c_in_t_kernel,
        grid=(m // tm,),
        in_specs=[row(d), _resident((1, d)), _resident(w.shape), _resident(wkv_t.shape)],
        out_specs=[row(cw), tr, tr, tail, tail],
        out_shape=[jax.ShapeDtypeStruct((m, cw), BF16)] + [jax.ShapeDtypeStruct((nb, cw, t), BF16)] * 2
        + [jax.ShapeDtypeStruct((nb, cw, keep), F32)] * 2,
        scratch_shapes=[pltpu.VMEM((tm, d), BF16)],
        compiler_params=_params("arbitrary"),
        name="c_in_proj_t",
    )(x, g, w, wkv_t)


def _bias_kernel(rb_ref, o_ref):
    rb = rb_ref[...]
    nh = rb.shape[0]
    hi = rb.astype(BF16)
    r1 = rb - hi.astype(F32)
    mid = r1.astype(BF16)
    lo = (r1 - mid.astype(F32)).astype(BF16)
    terms = jnp.concatenate([hi, mid, lo], axis=0)
    ncol, win = rb.shape[1], o_ref.shape[2]
    s = lax.broadcasted_iota(jnp.int32, (1, win), 1)
    kk = lax.broadcasted_iota(jnp.int32, (ncol, win), 0)

    def body(r, carry):
        idx = jnp.clip(BAND_PAST + r - s, -REL_CLIP, REL_CLIP) + REL_CLIP - REL_FIRST
        kc, qc = s >> CHUNK_SHIFT, r >> CHUNK_SHIFT
        visible = (kc >= qc) & (kc <= qc + LEFT_CHUNKS)
        onehot = jnp.where((kk == idx) & visible, 1.0, 0.0).astype(BF16)
        parts = _dot(terms, onehot)
        row = (parts[0:nh] + parts[nh:2 * nh]) + parts[2 * nh:3 * nh]
        o_ref[r] = jnp.where(visible, row, -jnp.inf)
        return carry

    lax.fori_loop(0, o_ref.shape[0], body, 0, unroll=BIAS_UNROLL)


def _bias_tile(rel_bias):
    h = rel_bias.shape[0]
    cols = rel_bias[:, REL_FIRST:]
    cols = jnp.pad(cols, ((0, 0), (0, REL_COLS - cols.shape[1])))
    tile = pl.pallas_call(
        _bias_kernel,
        out_shape=jax.ShapeDtypeStruct((BAND_Q, h, BAND_WIN), F32),
        compiler_params=pltpu.CompilerParams(vmem_limit_bytes=V7X_VMEM_LIMIT),
        name="band_bias_tile",
    )(cols)
    return jnp.transpose(tile, (1, 0, 2))


def _band_core(q_pairs, kwts, vwts, biases, valid, lane_lo, row_lo):
    bq = q_pairs[0].shape[0]
    zs = [_dot(_split_heads(q_pair, lane_lo), kwt) for q_pair, kwt in zip(q_pairs, kwts)]
    es, invs = [], []
    for z, (bias0, bias1), ok in zip(zs, biases, valid):
        s = z + jnp.concatenate([bias0, bias1], axis=0)
        if ok is not None:
            s = jnp.where(ok, s, -jnp.inf)
        e = jnp.exp2((s - jnp.max(s, axis=-1, keepdims=True)) * LOG2E)
        invs.append(1.0 / jnp.sum(e, axis=-1, keepdims=True))
        es.append(e.astype(BF16))
    outs = []
    for e, inv, vwt in zip(es, invs, vwts):
        o = _dot_t(jnp.concatenate([e[:bq], e[bq:]], axis=1), _split_heads_t(vwt, row_lo))
        outs.append((o * jnp.where(lane_lo, inv[:bq], inv[bq:])).astype(BF16))
    return outs


def _band_prompt_kernel(q_ref, kt_ref, vt_ref, bias_ref, o_ref, kpad_ref, vpad_ref):
    i = pl.program_id(2)
    bq, win = bias_ref.shape[1], bias_ref.shape[2]
    nblk = q_ref.shape[1] // bq
    npair = q_ref.shape[2] // V7X_LANES
    past = win - bq

    @pl.when(i == 0)
    def _():
        kpad_ref[:, 0:past] = jnp.zeros((kpad_ref.shape[0], past), BF16)
        vpad_ref[:, 0:past] = jnp.zeros((vpad_ref.shape[0], past), BF16)
        kpad_ref[:, past:] = kt_ref[0]
        vpad_ref[:, past:] = vt_ref[0]

    col = lax.broadcasted_iota(jnp.int32, (1, win), 1)
    work = [(j, p) for j in range(nblk) for p in range(npair)]
    cols = [pl.ds(pl.multiple_of((i * nblk + j) * bq, bq), win) for j in range(nblk)]

    def run(masked):
        valid = [col >= past - (i * nblk + j) * bq if masked else None for j in range(nblk)]
        outs = _band_core([q_ref[0, j * bq:(j + 1) * bq, _pair(p)] for j, p in work],
                          [kpad_ref[_pair(p), cols[j]] for j, p in work], [vpad_ref[_pair(p), cols[j]] for j, p in work],
                          [(bias_ref[2 * p], bias_ref[2 * p + 1]) for j, p in work], [valid[j] for j, p in work],
                          _lane_lo(), _row_lo())
        for (j, p), out in zip(work, outs):
            o_ref[0, j * bq:(j + 1) * bq, _pair(p)] = out

    lax.cond(i * nblk * bq < past, lambda: run(True), lambda: run(False))


def _band_prompt(q, kt, vt, bias):
    b, t, cw = q.shape
    bq, win = bias.shape[1], bias.shape[2]
    width = PAIRS_PER_STEP * V7X_LANES
    rows = BAND_BLOCKS_PER_STEP * bq
    qspec = pl.BlockSpec((1, rows, width), lambda p, bi, i: (bi, i, p))
    kvspec = pl.BlockSpec((1, width, t), lambda p, bi, i: (bi, p, 0))
    return pl.pallas_call(
        _band_prompt_kernel,
        grid=(cw // width, b, t // rows),
        in_specs=[qspec, kvspec, kvspec, pl.BlockSpec((2 * PAIRS_PER_STEP, bq, win), lambda p, bi, i: (p, 0, 0))],
        out_specs=qspec,
        out_shape=jax.ShapeDtypeStruct((b, t, cw), BF16),
        scratch_shapes=[pltpu.VMEM((width, win - bq + t), BF16)] * 2,
        compiler_params=_params("arbitrary", "arbitrary", "arbitrary"),
        name="band_attn_prompt",
    )(q, kt, vt, bias)


def _band_sample_kernel(q_ref, kc_ref, vc_ref, kn_ref, vn_ref, bias_ref, o_ref, kpad_ref, vpad_ref):
    t = q_ref.shape[1]
    npair = q_ref.shape[2] // V7X_LANES
    past = kc_ref.shape[3]
    win = bias_ref.shape[2]
    kpad_ref[...] = jnp.zeros_like(kpad_ref)
    vpad_ref[...] = jnp.zeros_like(vpad_ref)
    kpad_ref[0:t, :] = kn_ref[0]
    vpad_ref[0:t, :] = vn_ref[0]
    valid = lax.broadcasted_iota(jnp.int32, (1, win), 1) < past + t
    lane_lo, row_lo = _lane_lo(), _row_lo()
    pairs = range(npair)
    q2s = [_split_heads(q_ref[0, :, _pair(p)], lane_lo) for p in pairs]
    zs = [jnp.concatenate([_dot(q2s[p], _pair_rows(kc_ref, p)), _dot_t(q2s[p], kpad_ref[:, _pair(p)])], axis=1)
          for p in pairs]
    for p in pairs:
        s = zs[p] + jnp.concatenate([bias_ref[2 * p], bias_ref[2 * p + 1]], axis=0)
        s = jnp.where(valid, s, -jnp.inf)
        e = jnp.exp2((s - jnp.max(s, axis=-1, keepdims=True)) * LOG2E)
        inv = 1.0 / jnp.sum(e, axis=-1, keepdims=True)
        e = e.astype(BF16)
        e_cache = jnp.concatenate([e[:t, :past], e[t:, :past]], axis=1)
        e_new = jnp.concatenate([e[:t, past:], e[t:, past:]], axis=1)
        o = (_dot_t(e_cache, _split_heads_t(_pair_rows(vc_ref, p), row_lo))
             + _dot(e_new, _split_heads(vpad_ref[:, _pair(p)], lane_lo)))
        o_ref[0, :, _pair(p)] = (o * jnp.where(lane_lo, inv[:t], inv[t:])).astype(BF16)


def _band_sample(q, kc, vc, kn, vn, bias):
    b, t, cw = q.shape
    nh = bias.shape[0]
    past, win = kc.shape[3], bias.shape[2]
    new = pl.BlockSpec((1, t, cw), lambda bi: (bi, 0, 0))
    cache = pl.BlockSpec((1,) + kc.shape[1:], lambda bi: (bi, 0, 0, 0))
    return pl.pallas_call(
        _band_sample_kernel,
        grid=(b,),
        in_specs=[new, cache, cache, new, new, pl.BlockSpec((nh, t, win), lambda bi: (0, 0, 0))],
        out_specs=new,
        out_shape=jax.ShapeDtypeStruct((b, t, cw), BF16),
        scratch_shapes=[pltpu.VMEM((win - past, cw), BF16)] * 2,
        compiler_params=_params("arbitrary"),
        name="band_attn_sample",
    )(q, kc, vc, kn, vn, bias)


def kernel(x_prompt, x_sample, cache_sb_k, cache_sb_v, cache_conv, cache_band_k, cache_band_v, norm_mix, norm_ffn,
           norm_final, w_in_ab, w_out_ab, dw_w, dw_b, conv_ln_g, conv_ln_b, w_in_c, w_out_c, rel_bias, w_up, w_down):
    b, t, d = x_prompt.shape
    bs, ts, _ = x_sample.shape
    n_sb, n_c = cache_sb_k.shape[3], cache_band_k.shape[3]
    sw, cc, cw = n_sb * HEAD_DIM, dw_w.shape[2], n_c * HEAD_DIM
    past = cache_sb_k.shape[2]
    band_past = cache_band_k.shape[2]
    keep = min(BAND_PAST, t)
    assert w_in_ab.shape[0] == 1 and w_in_c.shape[0] == 1 and norm_mix.shape[0] == 2
    assert sw == cc and sw + cc == d and cw == d and w_in_ab.shape[2] == 3 * sw + 2 * cc
    assert t % ROW_TILE == 0 and ts % 16 == 0 and ts <= V7X_MXU_DIM
    assert past % V7X_MXU_DIM == 0 and band_past == BAND_PAST and band_past + ts <= BAND_WIN

    tm_p, tm_s = ROW_TILE, bs * ts
    vec = lambda a: a.reshape(1, -1)
    w_in0, w_out0 = w_in_ab[0].astype(BF16), w_out_ab[0].astype(BF16)
    w_in1, w_out1 = w_in_c[0].astype(BF16), w_out_c[0].astype(BF16)
    w_up_b, w_down_b = w_up.astype(BF16), w_down.astype(BF16)
    dw_pad = jnp.pad(dw_w[0], ((0, HALO_ROWS - CONV_WIDTH), (0, 0)))
    conv_vecs = (vec(dw_b[0]), vec(conv_ln_g[0]), vec(conv_ln_b[0]))
    bias = _bias_tile(rel_bias[0])

    shp = lambda z, n: z.reshape(b, t, n)
    x0 = x_prompt.reshape(b * t, d)
    wkv0_t = jnp.transpose(w_in0[:, sw:3 * sw])
    q, ktb, vtb, kp, vp, up = _ab_in_t(x0, vec(norm_mix[0]), w_in0, wkv0_t, sw, cc, t, tm_p)
    a = _sb_prompt(shp(q, sw), ktb, vtb).reshape(b * t, sw)
    c = _conv(shp(up, cc), shp(up, cc), dw_pad, *conv_vecs, CONV_TILE, False).reshape(b * t, cc)
    xp2 = _out_mlp(x0, a, c, 0, w_out0, vec(norm_ffn[0]), w_up_b[0], w_down_b[0], vec(norm_final), tm_p, False)
    wkv1_t = jnp.transpose(w_in1[:, cw:])
    q, ktb, vtb, bkp, bvp = _c_in_t(xp2, vec(norm_mix[1]), w_in1[:, :cw], wkv1_t, t, keep, tm_p)
    o = _band_prompt(shp(q, cw), ktb, vtb, bias).reshape(b * t, cw)
    yp = _out_mlp(xp2, o, o, 1, w_out1, vec(norm_ffn[1]), w_up_b[1], w_down_b[1], vec(norm_final), tm_p, True)
    by_time = lambda zt, nh: jnp.transpose(zt.reshape(b, nh, HEAD_DIM, -1), (0, 3, 1, 2))[None]

    def layer0(x, tm, attn, conv):
        q, kb, vb, k, v, u = _ab_in(x, vec(norm_mix[0]), w_in0, sw, cc, tm)
        a = attn(q, kb, vb)
        x2 = _out_mlp(x, a, conv(u), 0, w_out0, vec(norm_ffn[0]), w_up_b[0], w_down_b[0], vec(norm_final), tm, False)
        return x2, k, v, u

    def layer1(x, tm, tlen, tail, attn):
        q, k, v, kt, vt = _c_in(x, vec(norm_mix[1]), w_in1, tlen, tail, tm)
        o = attn(q, k, v)
        y = _out_mlp(x, o, o, 1, w_out1, vec(norm_ffn[1]), w_up_b[1], w_down_b[1], vec(norm_final), tm, True)
        return y, kt, vt

    shs = lambda z, n: z.reshape(bs, ts, n)
    conv_hist = jnp.pad(cache_conv[0], ((0, 0), (HALO_ROWS - CONV_STATE, 0), (0, 0)))
    by_head = lambda c: jnp.transpose(c[0], (0, 2, 3, 1))
    ck, cv = by_head(cache_sb_k), by_head(cache_sb_v)
    xs2, ks, vs, us = layer0(
        x_sample.reshape(bs * ts, d), tm_s,
        lambda q, k, v: _sb_sample(shs(q, sw), ck, cv, shs(k, sw), shs(v, sw)).reshape(bs * ts, sw),
        lambda u: _conv(shs(u, cc), conv_hist, dw_pad, *conv_vecs, ts, True).reshape(bs * ts, cc))
    bck, bcv = by_head(cache_band_k), by_head(cache_band_v)
    ys, bks, bvs = layer1(
        xs2, tm_s, ts, ts,
        lambda q, k, v: _band_sample(shs(q, cw), bck, bcv, shs(k, cw), shs(v, cw), bias).reshape(bs * ts, cw))

    new_conv_p = up.reshape(b, t, cc)[:, t - CONV_STATE:]
    new_conv_s = jnp.concatenate([cache_conv[0], us.reshape(bs, ts, cc)], axis=1)[:, ts:]
    return (yp.reshape(b, t, d), ys.reshape(bs, ts, d),
            by_time(kp, n_sb), by_time(vp, n_sb), new_conv_p[None], by_time(bkp, n_c), by_time(bvp, n_c),
            ks.reshape(1, bs, ts, n_sb, HEAD_DIM), vs.reshape(1, bs, ts, n_sb, HEAD_DIM), new_conv_s[None],
            bks.reshape(1, bs, ts, n_c, HEAD_DIM), bvs.reshape(1, bs, ts, n_c, HEAD_DIM))
```

```python
import functools

import jax
import jax.numpy as jnp
from jax import lax
from jax.experimental import pallas as pl
from jax.experimental.pallas import tpu as pltpu

F32 = jnp.float32
BF16 = jnp.bfloat16

HEAD_DIM = 64
CHUNK = 64
CHUNK_SHIFT = 6
LEFT_CHUNKS = 8
BAND_PAST = LEFT_CHUNKS * CHUNK
REL_CLIP = 128
CONV_WIDTH = 31
CONV_STATE = CONV_WIDTH - 1
RMS_EPS = 1e-6
LN_EPS = 1e-5
LOG2E = 1.4426950408889634
SCALE = HEAD_DIM ** -0.5

V7X_LANES = 128
SUBLANES = 8
V7X_MXU_DIM = 256
V7X_VMEM_LIMIT = 56 * 1024 * 1024
ROW_TILE = 512
FF_CHUNK = 512
CONV_TILE = 512
HALO_ROWS = 32
CONV_ROWS = 64
PAIRS_PER_STEP = 4
BAND_Q = 256
BAND_WIN = BAND_PAST + BAND_Q
BAND_BLOCKS_PER_STEP = 2
REL_FIRST = REL_CLIP + 1 - CHUNK
REL_COLS = 256
BIAS_UNROLL = 8


def _params(*semantics):
    return pltpu.CompilerParams(dimension_semantics=semantics, vmem_limit_bytes=V7X_VMEM_LIMIT)


def _dot(a, b):
    return jnp.dot(a, b, preferred_element_type=F32)


def _dot_t(a, b):
    return lax.dot_general(a, b, (((1,), (1,)), ((), ())), preferred_element_type=F32)


def _rms_rows(x, g):
    r = lax.rsqrt(jnp.mean(x * x, axis=-1, keepdims=True) + RMS_EPS)
    return (x * r) * g


def _resident(shape):
    return pl.BlockSpec(shape, lambda *_: (0,) * len(shape), pipeline_mode=pl.Buffered(1))


def _lane_lo():
    return lax.broadcasted_iota(jnp.int32, (1, V7X_LANES), 1) < HEAD_DIM


def _row_lo():
    return lax.broadcasted_iota(jnp.int32, (V7X_LANES, 1), 0) < HEAD_DIM


def _split_heads(xp, lane_lo):
    zero = jnp.zeros_like(xp)
    return jnp.concatenate([jnp.where(lane_lo, xp, zero), jnp.where(lane_lo, zero, xp)], axis=0)


def _split_heads_t(xt, row_lo):
    zero = jnp.zeros_like(xt)
    return jnp.concatenate([jnp.where(row_lo, xt, zero), jnp.where(row_lo, zero, xt)], axis=1)


def _pair(p):
    return slice(p * V7X_LANES, (p + 1) * V7X_LANES)


def _store_heads(dst_ref, src_ref):
    nh = dst_ref.shape[1]
    per_head = jnp.stack([src_ref[:, h * HEAD_DIM:(h + 1) * HEAD_DIM] for h in range(nh)], axis=0)
    dst_ref[...] = jnp.swapaxes(per_head, 0, 1)


def _ab_in_kernel(x_ref, g_ref, w_ref, q_ref, kb_ref, vb_ref, k_ref, v_ref, u_ref, h_ref, kf_ref, vf_ref):
    h_ref[...] = _rms_rows(x_ref[...], g_ref[...]).astype(BF16)
    sw = q_ref.shape[-1]
    cw = u_ref.shape[-1]
    proj = lambda c0, n: _dot(h_ref[...], w_ref[:, c0:c0 + n])
    q_ref[...] = (proj(0, sw) * SCALE).astype(BF16)
    for c0, b_ref, o_ref, f_ref in ((sw, kb_ref, k_ref, kf_ref), (2 * sw, vb_ref, v_ref, vf_ref)):
        f_ref[...] = proj(c0, sw)
        b_ref[...] = f_ref[...].astype(BF16)
        _store_heads(o_ref, f_ref)
    u_ref[...] = proj(3 * sw, cw) * jax.nn.sigmoid(proj(3 * sw + cw, cw))


def _ab_in(x, g, w, sw, cw, tm):
    m, d = x.shape
    nh = sw // HEAD_DIM
    row = lambda n: pl.BlockSpec((tm, n), lambda i: (i, 0))
    heads = pl.BlockSpec((tm, nh, HEAD_DIM), lambda i: (i, 0, 0))
    return pl.pallas_call(
        _ab_in_kernel,
        grid=(m // tm,),
        in_specs=[row(d), _resident((1, d)), _resident(w.shape)],
        out_specs=[row(sw)] * 3 + [heads] * 2 + [row(cw)],
        out_shape=[jax.ShapeDtypeStruct((m, sw), BF16)] * 3 + [jax.ShapeDtypeStruct((m, nh, HEAD_DIM), F32)] * 2
        + [jax.ShapeDtypeStruct((m, cw), F32)],
        scratch_shapes=[pltpu.VMEM((tm, d), BF16)] + [pltpu.VMEM((tm, sw), F32)] * 2,
        compiler_params=_params("arbitrary"),
        name="ab_in_proj",
    )(x, g, w)


def _ab_in_t_kernel(x_ref, g_ref, w_ref, wkv_ref, q_ref, ktb_ref, vtb_ref, kt_ref, vt_ref, u_ref, h_ref):
    h_ref[...] = _rms_rows(x_ref[...], g_ref[...]).astype(BF16)
    sw = q_ref.shape[-1]
    cw = u_ref.shape[-1]
    proj = lambda c0, n: _dot(h_ref[...], w_ref[:, c0:c0 + n])
    q_ref[...] = (proj(0, sw) * SCALE).astype(BF16)
    for r0, b_ref, f_ref in ((0, ktb_ref, kt_ref), (sw, vtb_ref, vt_ref)):
        f_ref[0] = _dot_t(wkv_ref[r0:r0 + sw, :], h_ref[...])
        b_ref[0] = f_ref[0].astype(BF16)
    u_ref[...] = proj(3 * sw, cw) * jax.nn.sigmoid(proj(3 * sw + cw, cw))


def _ab_in_t(x, g, w, wkv_t, sw, cw, t, tm):
    m, d = x.shape
    per = t // tm
    row = lambda n: pl.BlockSpec((tm, n), lambda i: (i, 0))
    tr = pl.BlockSpec((1, sw, tm), lambda i: (i // per, 0, i % per))
    tshape = lambda dt: jax.ShapeDtypeStruct((m // t, sw, t), dt)
    return pl.pallas_call(
        _ab_in_t_kernel,
        grid=(m // tm,),
        in_specs=[row(d), _resident((1, d)), _resident(w.shape), _resident(wkv_t.shape)],
        out_specs=[row(sw), tr, tr, tr, tr, row(cw)],
        out_shape=[jax.ShapeDtypeStruct((m, sw), BF16), tshape(BF16), tshape(BF16), tshape(F32), tshape(F32),
                   jax.ShapeDtypeStruct((m, cw), F32)],
        scratch_shapes=[pltpu.VMEM((tm, d), BF16)],
        compiler_params=_params("arbitrary"),
        name="ab_in_proj_t",
    )(x, g, w, wkv_t)


def _softplus(z):
    return jnp.maximum(z, 0.0) + jnp.log(1.0 + jnp.exp2(jnp.abs(z) * -LOG2E))


def _sb_scores(q2s, ktblks):
    return [_dot(q2, ktblk) for q2, ktblk in zip(q2s, ktblks)]


def _sb_cumsums(zs, tri2, mask):
    cs = []
    for z in zs:
        sp = _softplus(z)
        if mask is not None:
            sp = jnp.where(mask, sp, 0.0)
        hi = sp.astype(BF16)
        lo = (sp - hi.astype(F32)).astype(BF16)
        cs.append(_dot(jnp.concatenate([hi, lo], axis=1), tri2))
    return cs


def _sb_weigh(zs, cs, vtblks, states, mask, row_lo):
    out = []
    for z, c, vtblk, (carry, acc) in zip(zs, cs, vtblks, states):
        bq = acc.shape[0]
        w = jnp.exp2((z - c - carry) * LOG2E)
        if mask is not None:
            w = jnp.where(mask, w, 0.0)
        w = w.astype(BF16)
        wcat = jnp.concatenate([w[:bq], w[bq:]], axis=1)
        out.append((carry + c[:, 0:1], acc + _dot_t(wcat, _split_heads_t(vtblk, row_lo))))
    return tuple(out)


def _sb_blocks(q2s, ktblks, vtblks, tri2, states, mask, row_lo):
    zs = _sb_scores(q2s, ktblks)
    return _sb_weigh(zs, _sb_cumsums(zs, tri2, mask), vtblks, states, mask, row_lo)


def _sb_consts(bq, kb):
    tri = jnp.where((lax.broadcasted_iota(jnp.int32, (2 * kb, kb), 0) & (kb - 1))
                    >= lax.broadcasted_iota(jnp.int32, (2 * kb, kb), 1), 1.0, 0.0).astype(BF16)
    assert bq & (bq - 1) == 0
    t_loc = lax.broadcasted_iota(jnp.int32, (2 * bq, kb), 0) & (bq - 1)
    causal = lax.broadcasted_iota(jnp.int32, (2 * bq, kb), 1) < t_loc
    return _lane_lo(), tri, causal


def _sb_zero_state(bq):
    return jnp.zeros((2 * bq, 1), F32), jnp.zeros((bq, V7X_LANES), F32)


def _sb_prompt_kernel(q_ref, kt_ref, vt_ref, o_ref):
    i = pl.program_id(2)
    bq = q_ref.shape[1]
    npair = q_ref.shape[2] // V7X_LANES
    lane_lo, tri, causal = _sb_consts(bq, bq)
    row_lo = _row_lo()
    q2 = [_split_heads(q_ref[0, :, _pair(p)], lane_lo) for p in range(npair)]

    def blocks(j):
        cols = pl.ds(pl.multiple_of(j * bq, bq), bq)
        return ([kt_ref[0, _pair(p), cols] for p in range(npair)], [vt_ref[0, _pair(p), cols] for p in range(npair)])

    def sweep(j, states, mask):
        return _sb_blocks(q2, *blocks(j), tri, states, mask, row_lo)

    def sweep2(s, states):
        j = i - 1 - 2 * s
        (ka, va), (kb, vb) = blocks(j), blocks(j - 1)
        z_a = _sb_scores(q2, ka)
        z_b = _sb_scores(q2, kb)
        c_a = _sb_cumsums(z_a, tri, None)
        c_b = _sb_cumsums(z_b, tri, None)
        states = _sb_weigh(z_a, c_a, va, states, None, row_lo)
        return _sb_weigh(z_b, c_b, vb, states, None, row_lo)

    states = sweep(i, (_sb_zero_state(bq),) * npair, causal)
    states = lax.fori_loop(0, i // 2, sweep2, states)
    states = lax.cond(i % 2 == 1, lambda st: sweep(0, st, None), lambda st: st, states)
    for p in range(npair):
        o_ref[0, :, _pair(p)] = states[p][1].astype(BF16)


def _sb_prompt(q, kt, vt):
    b, t, sw = q.shape
    bq = V7X_MXU_DIM
    width = PAIRS_PER_STEP * V7X_LANES
    qspec = pl.BlockSpec((1, bq, width), lambda bi, p, i: (bi, i, p))
    kvspec = pl.BlockSpec((1, width, t), lambda bi, p, i: (bi, p, 0))
    return pl.pallas_call(
        _sb_prompt_kernel,
        grid=(b, sw // width, t // bq),
        in_specs=[qspec, kvspec, kvspec],
        out_specs=qspec,
        out_shape=jax.ShapeDtypeStruct((b, t, sw), BF16),
        compiler_params=_params("arbitrary", "arbitrary", "arbitrary"),
        name="sb_attn_prompt",
    )(q, kt, vt)


def _pair_rows(ref, p):
    return jnp.concatenate([ref[0, 2 * p], ref[0, 2 * p + 1]], axis=0).astype(BF16)


def _sb_sample_kernel(q_ref, kc_ref, vc_ref, kn_ref, vn_ref, o_ref, kpad_ref, vpad_ref):
    bq = q_ref.shape[1]
    kb = kpad_ref.shape[0]
    npair = q_ref.shape[2] // V7X_LANES
    nblk = kc_ref.shape[3] // kb
    lane_lo, tri2, causal = _sb_consts(bq, kb)
    row_lo = _row_lo()
    kpad_ref[...] = jnp.zeros_like(kpad_ref)
    vpad_ref[...] = jnp.zeros_like(vpad_ref)
    kpad_ref[0:bq, :] = kn_ref[0]
    vpad_ref[0:bq, :] = vn_ref[0]
    cols = [slice(j * kb, (j + 1) * kb) for j in reversed(range(nblk))]
    for p in range(npair):
        q2 = _split_heads(q_ref[0, :, _pair(p)], lane_lo)
        kt, vt = _pair_rows(kc_ref, p), _pair_rows(vc_ref, p)
        z = [_dot_t(q2, kpad_ref[:, _pair(p)])] + [_dot(q2, kt[:, c]) for c in cols]
        sp = [_softplus(zb) for zb in z]
        sp[0] = jnp.where(causal, sp[0], 0.0)
        sp_all = jnp.concatenate(sp, axis=0)
        hi = sp_all.astype(BF16)
        lo = (sp_all - hi.astype(F32)).astype(BF16)
        c_all = _dot(jnp.concatenate([hi, lo], axis=1), tri2)
        carry = jnp.zeros((2 * bq, 1), F32)
        ws = []
        for j, zb in enumerate(z):
            c = c_all[j * 2 * bq:(j + 1) * 2 * bq]
            w = jnp.exp2((zb - c - carry) * LOG2E)
            if j == 0:
                w = jnp.where(causal, w, 0.0)
            carry = carry + c[:, 0:1]
            w = w.astype(BF16)
            ws.append(jnp.concatenate([w[:bq], w[bq:]], axis=1))
        vt_cat = jnp.concatenate([_split_heads_t(vt[:, c], row_lo) for c in cols], axis=1)
        acc = _dot(ws[0], _split_heads(vpad_ref[:, _pair(p)], lane_lo)) + _dot_t(jnp.concatenate(ws[1:], axis=1), vt_cat)
        o_ref[0, :, _pair(p)] = acc.astype(BF16)


def _sb_sample(q, kc, vc, kn, vn):
    b, t, sw = q.shape
    new = pl.BlockSpec((1, t, sw), lambda bi: (bi, 0, 0))
    cache = pl.BlockSpec((1,) + kc.shape[1:], lambda bi: (bi, 0, 0, 0))
    return pl.pallas_call(
        _sb_sample_kernel,
        grid=(b,),
        in_specs=[new, cache, cache, new, new],
        out_specs=new,
        out_shape=jax.ShapeDtypeStruct((b, t, sw), BF16),
        scratch_shapes=[pltpu.VMEM((V7X_MXU_DIM, sw), BF16)] * 2,
        compiler_params=_params("arbitrary"),
        name="sb_attn_sample",
    )(q, kc, vc, kn, vn)


def _conv_kernel(u_ref, halo_ref, dw_ref, b_ref, g_ref, be_ref, c_ref, ext_ref, sh_ref, *, halo_is_history):
    tm = u_ref.shape[1]
    halo = halo_ref[0]
    if not halo_is_history:
        halo = jnp.where(pl.program_id(1) > 0, halo, 0.0)
    ext_ref[0:HALO_ROWS, :] = halo
    ext_ref[HALO_ROWS:, :] = u_ref[0]
    n = sh_ref.shape[1]
    for s in range(1, SUBLANES):
        sh_ref[s - 1] = ext_ref[s:s + n, :]
    first = HALO_ROWS - CONV_STATE
    rows = min(CONV_ROWS, tm)
    for r0 in range(0, tm, rows):
        y = b_ref[...]
        for w in range(CONV_WIDTH):
            s = (first + w) % SUBLANES
            a = r0 + first + w - s
            tap = ext_ref[a:a + rows, :] if s == 0 else sh_ref[s - 1, a:a + rows, :]
            y = y + tap * dw_ref[w:w + 1, :]
        mu = jnp.mean(y, axis=-1, keepdims=True)
        var = jnp.mean(jnp.square(y - mu), axis=-1, keepdims=True)
        yn = ((y - mu) * lax.rsqrt(var + LN_EPS)) * g_ref[...] + be_ref[...]
        c_ref[0, r0:r0 + rows, :] = (yn * jax.nn.sigmoid(yn)).astype(BF16)


def _conv(u, halo_src, dw, b, g, be, tm, halo_is_history):
    bsz, t, c = u.shape
    if halo_is_history:
        halo_spec = pl.BlockSpec((1, HALO_ROWS, c), lambda bi, ti: (bi, 0, 0))
    else:
        per = tm // HALO_ROWS
        halo_spec = pl.BlockSpec((1, HALO_ROWS, c), lambda bi, ti: (bi, jnp.maximum(ti * per - 1, 0), 0))
    tile = pl.BlockSpec((1, tm, c), lambda bi, ti: (bi, ti, 0))
    vec = _resident((1, c))
    return pl.pallas_call(
        functools.partial(_conv_kernel, halo_is_history=halo_is_history),
        grid=(bsz, t // tm),
        in_specs=[tile, halo_spec, _resident(dw.shape), vec, vec, vec],
        out_specs=tile,
        out_shape=jax.ShapeDtypeStruct((bsz, t, c), BF16),
        scratch_shapes=[pltpu.VMEM((HALO_ROWS + tm, c), F32),
                        pltpu.VMEM((SUBLANES - 1, HALO_ROWS + tm - SUBLANES, c), F32)],
        compiler_params=_params("arbitrary", "arbitrary"),
        name="conv_module",
    )(u, halo_src, dw, b, g, be)


def _out_mlp_kernel(x_ref, m0_ref, m1_ref, wo_ref, g_ref, wu_ref, wd_ref, gf_ref, y_ref, h_ref, *, final_norm):
    half = m0_ref.shape[-1]
    x1 = x_ref[...] + _dot(m0_ref[...], wo_ref[0:half, :]) + _dot(m1_ref[...], wo_ref[half:, :])
    h_ref[...] = _rms_rows(x1, g_ref[...]).astype(BF16)
    y_ref[...] = x1
    for c0 in range(0, wu_ref.shape[1], FF_CHUNK):
        up = _dot(h_ref[...], wu_ref[:, c0:c0 + FF_CHUNK])
        act = jnp.square(jnp.maximum(up, 0.0)).astype(BF16)
        y_ref[...] += _dot(act, wd_ref[c0:c0 + FF_CHUNK, :])
    if final_norm:
        y_ref[...] = _rms_rows(y_ref[...], gf_ref[...])


def _out_mlp(x, m0, m1, m1_block, wo, g, wu, wd, gf, tm, final_norm):
    m, d = x.shape
    half = wo.shape[0] // 2
    row = pl.BlockSpec((tm, d), lambda i: (i, 0))
    return pl.pallas_call(
        functools.partial(_out_mlp_kernel, final_norm=final_norm),
        grid=(m // tm,),
        in_specs=[row, pl.BlockSpec((tm, half), lambda i: (i, 0)), pl.BlockSpec((tm, half), lambda i: (i, m1_block)),
                  _resident(wo.shape), _resident((1, d)), _resident(wu.shape), _resident(wd.shape), _resident((1, d))],
        out_specs=row,
        out_shape=jax.ShapeDtypeStruct((m, d), F32),
        scratch_shapes=[pltpu.VMEM((tm, d), BF16)],
        compiler_params=_params("arbitrary"),
        name="out_proj_mlp",
    )(x, m0, m1, wo, g, wu, wd, gf)


def _c_in_kernel(x_ref, g_ref, w_ref, q_ref, k_ref, v_ref, kt_ref, vt_ref, h_ref, kf_ref, vf_ref, *, tail_every):
    h_ref[...] = _rms_rows(x_ref[...], g_ref[...]).astype(BF16)
    cw = q_ref.shape[-1]
    half = cw // 2
    for c0 in range(0, cw, half):
        cols = slice(c0, c0 + half)
        q_ref[:, cols] = (_dot(h_ref[...], w_ref[:, c0:c0 + half]) * SCALE).astype(BF16)
        for base, b_ref, f_ref in ((cw, k_ref, kf_ref), (2 * cw, v_ref, vf_ref)):
            f_ref[:, cols] = _dot(h_ref[...], w_ref[:, base + c0:base + c0 + half])
            b_ref[:, cols] = f_ref[:, cols].astype(BF16)

    @pl.when((pl.program_id(0) + 1) % tail_every == 0)
    def _():
        _store_heads(kt_ref, kf_ref)
        _store_heads(vt_ref, vf_ref)


def _c_in(x, g, w, t, keep, tm):
    m, d = x.shape
    cw = w.shape[1] // 3
    nh = cw // HEAD_DIM
    row = lambda n: pl.BlockSpec((tm, n), lambda i: (i, 0))
    if keep == t:
        tail_every = 1
        tail_map = lambda i: (i, 0, 0)
    else:
        assert keep == tm and t % tm == 0
        tail_every = t // tm
        tail_map = lambda i: (i // tail_every, 0, 0)
    tail = pl.BlockSpec((tm, nh, HEAD_DIM), tail_map)
    nb = m // t
    return pl.pallas_call(
        functools.partial(_c_in_kernel, tail_every=tail_every),
        grid=(m // tm,),
        in_specs=[row(d), _resident((1, d)), _resident(w.shape)],
        out_specs=[row(cw), row(cw), row(cw), tail, tail],
        out_shape=[jax.ShapeDtypeStruct((m, cw), BF16)] * 3 + [jax.ShapeDtypeStruct((nb * keep, nh, HEAD_DIM), F32)] * 2,
        scratch_shapes=[pltpu.VMEM((tm, d), BF16)] + [pltpu.VMEM((tm, cw), F32)] * 2,
        compiler_params=_params("arbitrary"),
        name="c_in_proj",
    )(x, g, w)


def _c_in_t_kernel(x_ref, g_ref, w_ref, wkv_ref, q_ref, ktb_ref, vtb_ref, kt_ref, vt_ref, h_ref):
    h_ref[...] = _rms_rows(x_ref[...], g_ref[...]).astype(BF16)
    cw = q_ref.shape[-1]
    half = cw // 2
    for c0 in range(0, cw, half):
        q_ref[:, c0:c0 + half] = (_dot(h_ref[...], w_ref[:, c0:c0 + half]) * SCALE).astype(BF16)
    for base, b_ref, f_ref in ((0, ktb_ref, kt_ref), (cw, vtb_ref, vt_ref)):
        for r0 in range(0, cw, half):
            f_ref[0, r0:r0 + half, :] = _dot_t(wkv_ref[base + r0:base + r0 + half, :], h_ref[...])
            b_ref[0, r0:r0 + half, :] = f_ref[0, r0:r0 + half, :].astype(BF16)


def _c_in_t(x, g, w, wkv_t, t, keep, tm):
    m, d = x.shape
    cw = w.shape[1]
    assert keep == tm and t % tm == 0
    per = t // tm
    nb = m // t
    row = lambda n: pl.BlockSpec((tm, n), lambda i: (i, 0))
    tr = pl.BlockSpec((1, cw, tm), lambda i: (i // per, 0, i % per))
    tail = pl.BlockSpec((1, cw, keep), lambda i: (i // per, 0, 0))
    return pl.pallas_call(
        _c_in_t_kernel,
        grid=(m // tm,),
        in_specs=[row(d), _resident((1, d)), _resident(w.shape), _resident(wkv_t.shape)],
        out_specs=[row(cw), tr, tr, tail, tail],
        out_shape=[jax.ShapeDtypeStruct((m, cw), BF16)] + [jax.ShapeDtypeStruct((nb, cw, t), BF16)] * 2
        + [jax.ShapeDtypeStruct((nb, cw, keep), F32)] * 2,
        scratch_shapes=[pltpu.VMEM((tm, d), BF16)],
        compiler_params=_params("arbitrary"),
        name="c_in_proj_t",
    )(x, g, w, wkv_t)


def _bias_kernel(rb_ref, o_ref):
    rb = rb_ref[...]
    nh = rb.shape[0]
    hi = rb.astype(BF16)
    r1 = rb - hi.astype(F32)
    mid = r1.astype(BF16)
    lo = (r1 - mid.astype(F32)).astype(BF16)
    terms = jnp.concatenate([hi, mid, lo], axis=0)
    ncol, win = rb.shape[1], o_ref.shape[2]
    s = lax.broadcasted_iota(jnp.int32, (1, win), 1)
    kk = lax.broadcasted_iota(jnp.int32, (ncol, win), 0)

    def body(r, carry):
        idx = jnp.clip(BAND_PAST + r - s, -REL_CLIP, REL_CLIP) + REL_CLIP - REL_FIRST
        kc, qc = s >> CHUNK_SHIFT, r >> CHUNK_SHIFT
        visible = (kc >= qc) & (kc <= qc + LEFT_CHUNKS)
        onehot = jnp.where((kk == idx) & visible, 1.0, 0.0).astype(BF16)
        parts = _dot(terms, onehot)
        row = (parts[0:nh] + parts[nh:2 * nh]) + parts[2 * nh:3 * nh]
        o_ref[r] = jnp.where(visible, row, -jnp.inf)
        return carry

    lax.fori_loop(0, o_ref.shape[0], body, 0, unroll=BIAS_UNROLL)


def _bias_tile(rel_bias):
    h = rel_bias.shape[0]
    cols = rel_bias[:, REL_FIRST:]
    cols = jnp.pad(cols, ((0, 0), (0, REL_COLS - cols.shape[1])))
    tile = pl.pallas_call(
        _bias_kernel,
        out_shape=jax.ShapeDtypeStruct((BAND_Q, h, BAND_WIN), F32),
        compiler_params=pltpu.CompilerParams(vmem_limit_bytes=V7X_VMEM_LIMIT),
        name="band_bias_tile",
    )(cols)
    return jnp.transpose(tile, (1, 0, 2))


def _band_core(q_pairs, kwts, vwts, biases, valid, lane_lo, row_lo):
    bq = q_pairs[0].shape[0]
    zs = [_dot(_split_heads(q_pair, lane_lo), kwt) for q_pair, kwt in zip(q_pairs, kwts)]
    es, invs = [], []
    for z, (bias0, bias1), ok in zip(zs, biases, valid):
        s = z + jnp.concatenate([bias0, bias1], axis=0)
        if ok is not None:
            s = jnp.where(ok, s, -jnp.inf)
        e = jnp.exp2((s - jnp.max(s, axis=-1, keepdims=True)) * LOG2E)
        invs.append(1.0 / jnp.sum(e, axis=-1, keepdims=True))
        es.append(e.astype(BF16))
    outs = []
    for e, inv, vwt in zip(es, invs, vwts):
        o = _dot_t(jnp.concatenate([e[:bq], e[bq:]], axis=1), _split_heads_t(vwt, row_lo))
        outs.append((o * jnp.where(lane_lo, inv[:bq], inv[bq:])).astype(BF16))
    return outs


def _band_prompt_kernel(q_ref, kt_ref, vt_ref, bias_ref, o_ref, kpad_ref, vpad_ref):
    i = pl.program_id(2)
    bq, win = bias_ref.shape[1], bias_ref.shape[2]
    nblk = q_ref.shape[1] // bq
    npair = q_ref.shape[2] // V7X_LANES
    past = win - bq

    @pl.when(i == 0)
    def _():
        kpad_ref[:, 0:past] = jnp.zeros((kpad_ref.shape[0], past), BF16)
        vpad_ref[:, 0:past] = jnp.zeros((vpad_ref.shape[0], past), BF16)
        kpad_ref[:, past:] = kt_ref[0]
        vpad_ref[:, past:] = vt_ref[0]

    col = lax.broadcasted_iota(jnp.int32, (1, win), 1)
    work = [(j, p) for j in range(nblk) for p in range(npair)]
    cols = [pl.ds(pl.multiple_of((i * nblk + j) * bq, bq), win) for j in range(nblk)]

    def run(masked):
        valid = [col >= past - (i * nblk + j) * bq if masked else None for j in range(nblk)]
        outs = _band_core([q_ref[0, j * bq:(j + 1) * bq, _pair(p)] for j, p in work],
                          [kpad_ref[_pair(p), cols[j]] for j, p in work], [vpad_ref[_pair(p), cols[j]] for j, p in work],
                          [(bias_ref[2 * p], bias_ref[2 * p + 1]) for j, p in work], [valid[j] for j, p in work],
                          _lane_lo(), _row_lo())
        for (j, p), out in zip(work, outs):
            o_ref[0, j * bq:(j + 1) * bq, _pair(p)] = out

    lax.cond(i * nblk * bq < past, lambda: run(True), lambda: run(False))


def _band_prompt(q, kt, vt, bias):
    b, t, cw = q.shape
    bq, win = bias.shape[1], bias.shape[2]
    width = PAIRS_PER_STEP * V7X_LANES
    rows = BAND_BLOCKS_PER_STEP * bq
    qspec = pl.BlockSpec((1, rows, width), lambda p, bi, i: (bi, i, p))
    kvspec = pl.BlockSpec((1, width, t), lambda p, bi, i: (bi, p, 0))
    return pl.pallas_call(
        _band_prompt_kernel,
        grid=(cw // width, b, t // rows),
        in_specs=[qspec, kvspec, kvspec, pl.BlockSpec((2 * PAIRS_PER_STEP, bq, win), lambda p, bi, i: (p, 0, 0))],
        out_specs=qspec,
        out_shape=jax.ShapeDtypeStruct((b, t, cw), BF16),
        scratch_shapes=[pltpu.VMEM((width, win - bq + t), BF16)] * 2,
        compiler_params=_params("arbitrary", "arbitrary", "arbitrary"),
        name="band_attn_prompt",
    )(q, kt, vt, bias)


def _band_sample_kernel(q_ref, kc_ref, vc_ref, kn_ref, vn_ref, bias_ref, o_ref, kpad_ref, vpad_ref):
    t = q_ref.shape[1]
    npair = q_ref.shape[2] // V7X_LANES
    past = kc_ref.shape[3]
    win = bias_ref.shape[2]
    kpad_ref[...] = jnp.zeros_like(kpad_ref)
    vpad_ref[...] = jnp.zeros_like(vpad_ref)
    kpad_ref[0:t, :] = kn_ref[0]
    vpad_ref[0:t, :] = vn_ref[0]
    valid = lax.broadcasted_iota(jnp.int32, (1, win), 1) < past + t
    lane_lo, row_lo = _lane_lo(), _row_lo()
    pairs = range(npair)
    q2s = [_split_heads(q_ref[0, :, _pair(p)], lane_lo) for p in pairs]
    zs = [jnp.concatenate([_dot(q2s[p], _pair_rows(kc_ref, p)), _dot_t(q2s[p], kpad_ref[:, _pair(p)])], axis=1)
          for p in pairs]
    for p in pairs:
        s = zs[p] + jnp.concatenate([bias_ref[2 * p], bias_ref[2 * p + 1]], axis=0)
        s = jnp.where(valid, s, -jnp.inf)
        e = jnp.exp2((s - jnp.max(s, axis=-1, keepdims=True)) * LOG2E)
        inv = 1.0 / jnp.sum(e, axis=-1, keepdims=True)
        e = e.astype(BF16)
        e_cache = jnp.concatenate([e[:t, :past], e[t:, :past]], axis=1)
        e_new = jnp.concatenate([e[:t, past:], e[t:, past:]], axis=1)
        o = (_dot_t(e_cache, _split_heads_t(_pair_rows(vc_ref, p), row_lo))
             + _dot(e_new, _split_heads(vpad_ref[:, _pair(p)], lane_lo)))
        o_ref[0, :, _pair(p)] = (o * jnp.where(lane_lo, inv[:t], inv[t:])).astype(BF16)


def _band_sample(q, kc, vc, kn, vn, bias):
    b, t, cw = q.shape
    nh = bias.shape[0]
    past, win = kc.shape[3], bias.shape[2]
    new = pl.BlockSpec((1, t, cw), lambda bi: (bi, 0, 0))
    cache = pl.BlockSpec((1,) + kc.shape[1:], lambda bi: (bi, 0, 0, 0))
    return pl.pallas_call(
        _band_sample_kernel,
        grid=(b,),
        in_specs=[new, cache, cache, new, new, pl.BlockSpec((nh, t, win), lambda bi: (0, 0, 0))],
        out_specs=new,
        out_shape=jax.ShapeDtypeStruct((b, t, cw), BF16),
        scratch_shapes=[pltpu.VMEM((win - past, cw), BF16)] * 2,
        compiler_params=_params("arbitrary"),
        name="band_attn_sample",
    )(q, kc, vc, kn, vn, bias)


def kernel(x_prompt, x_sample, cache_sb_k, cache_sb_v, cache_conv, cache_band_k, cache_band_v, norm_mix, norm_ffn,
           norm_final, w_in_ab, w_out_ab, dw_w, dw_b, conv_ln_g, conv_ln_b, w_in_c, w_out_c, rel_bias, w_up, w_down):
    b, t, d = x_prompt.shape
    bs, ts, _ = x_sample.shape
    n_sb, n_c = cache_sb_k.shape[3], cache_band_k.shape[3]
    sw, cc, cw = n_sb * HEAD_DIM, dw_w.shape[2], n_c * HEAD_DIM
    past = cache_sb_k.shape[2]
    band_past = cache_band_k.shape[2]
    keep = min(BAND_PAST, t)
    assert w_in_ab.shape[0] == 1 and w_in_c.shape[0] == 1 and norm_mix.shape[0] == 2
    assert sw == cc and sw + cc == d and cw == d and w_in_ab.shape[2] == 3 * sw + 2 * cc
    assert t % ROW_TILE == 0 and ts % 16 == 0 and ts <= V7X_MXU_DIM
    assert past % V7X_MXU_DIM == 0 and band_past == BAND_PAST and band_past + ts <= BAND_WIN

    tm_p, tm_s = ROW_TILE, bs * ts
    vec = lambda a: a.reshape(1, -1)
    w_in0, w_out0 = w_in_ab[0].astype(BF16), w_out_ab[0].astype(BF16)
    w_in1, w_out1 = w_in_c[0].astype(BF16), w_out_c[0].astype(BF16)
    w_up_b, w_down_b = w_up.astype(BF16), w_down.astype(BF16)
    dw_pad = jnp.pad(dw_w[0], ((0, HALO_ROWS - CONV_WIDTH), (0, 0)))
    conv_vecs = (vec(dw_b[0]), vec(conv_ln_g[0]), vec(conv_ln_b[0]))
    bias = _bias_tile(rel_bias[0])

    shp = lambda z, n: z.reshape(b, t, n)
    x0 = x_prompt.reshape(b * t, d)
    wkv0_t = jnp.transpose(w_in0[:, sw:3 * sw])
    q, ktb, vtb, kp, vp, up = _ab_in_t(x0, vec(norm_mix[0]), w_in0, wkv0_t, sw, cc, t, tm_p)
    a = _sb_prompt(shp(q, sw), ktb, vtb).reshape(b * t, sw)
    c = _conv(shp(up, cc), shp(up, cc), dw_pad, *conv_vecs, CONV_TILE, False).reshape(b * t, cc)
    xp2 = _out_mlp(x0, a, c, 0, w_out0, vec(norm_ffn[0]), w_up_b[0], w_down_b[0], vec(norm_final), tm_p, False)
    wkv1_t = jnp.transpose(w_in1[:, cw:])
    q, ktb, vtb, bkp, bvp = _c_in_t(xp2, vec(norm_mix[1]), w_in1[:, :cw], wkv1_t, t, keep, tm_p)
    o = _band_prompt(shp(q, cw), ktb, vtb, bias).reshape(b * t, cw)
    yp = _out_mlp(xp2, o, o, 1, w_out1, vec(norm_ffn[1]), w_up_b[1], w_down_b[1], vec(norm_final), tm_p, True)
    by_time = lambda zt, nh: jnp.transpose(zt.reshape(b, nh, HEAD_DIM, -1), (0, 3, 1, 2))[None]

    def layer0(x, tm, attn, conv):
        q, kb, vb, k, v, u = _ab_in(x, vec(norm_mix[0]), w_in0, sw, cc, tm)
        a = attn(q, kb, vb)
        x2 = _out_mlp(x, a, conv(u), 0, w_out0, vec(norm_ffn[0]), w_up_b[0], w_down_b[0], vec(norm_final), tm, False)
        return x2, k, v, u

    def layer1(x, tm, tlen, tail, attn):
        q, k, v, kt, vt = _c_in(x, vec(norm_mix[1]), w_in1, tlen, tail, tm)
        o = attn(q, k, v)
        y = _out_mlp(x, o, o, 1, w_out1, vec(norm_ffn[1]), w_up_b[1], w_down_b[1], vec(norm_final), tm, True)
        return y, kt, vt

    shs = lambda z, n: z.reshape(bs, ts, n)
    conv_hist = jnp.pad(cache_conv[0], ((0, 0), (HALO_ROWS - CONV_STATE, 0), (0, 0)))
    by_head = lambda c: jnp.transpose(c[0], (0, 2, 3, 1))
    ck, cv = by_head(cache_sb_k), by_head(cache_sb_v)
    xs2, ks, vs, us = layer0(
        x_sample.reshape(bs * ts, d), tm_s,
        lambda q, k, v: _sb_sample(shs(q, sw), ck, cv, shs(k, sw), shs(v, sw)).reshape(bs * ts, sw),
        lambda u: _conv(shs(u, cc), conv_hist, dw_pad, *conv_vecs, ts, True).reshape(bs * ts, cc))
    bck, bcv = by_head(cache_band_k), by_head(cache_band_v)
    ys, bks, bvs = layer1(
        xs2, tm_s, ts, ts,
        lambda q, k, v: _band_sample(shs(q, cw), bck, bcv, shs(k, cw), shs(v, cw), bias).reshape(bs * ts, cw))

    new_conv_p = up.reshape(b, t, cc)[:, t - CONV_STATE:]
    new_conv_s = jnp.concatenate([cache_conv[0], us.reshape(bs, ts, cc)], axis=1)[:, ts:]
    return (yp.reshape(b, t, d), ys.reshape(bs, ts, d),
            by_time(kp, n_sb), by_time(vp, n_sb), new_conv_p[None], by_time(bkp, n_c), by_time(bvp, n_c),
            ks.reshape(1, bs, ts, n_sb, HEAD_DIM), vs.reshape(1, bs, ts, n_sb, HEAD_DIM), new_conv_s[None],
            bks.reshape(1, bs, ts, n_c, HEAD_DIM), bvs.reshape(1, bs, ts, n_c, HEAD_DIM))
```

```python
import functools

import jax
import jax.numpy as jnp
from jax import lax
from jax.experimental import pallas as pl
from jax.experimental.pallas import tpu as pltpu

F32 = jnp.float32
BF16 = jnp.bfloat16

HEAD_DIM = 64
CHUNK = 64
CHUNK_SHIFT = 6
LEFT_CHUNKS = 8
BAND_PAST = LEFT_CHUNKS * CHUNK
REL_CLIP = 128
CONV_WIDTH = 31
CONV_STATE = CONV_WIDTH - 1
RMS_EPS = 1e-6
LN_EPS = 1e-5
LOG2E = 1.4426950408889634
SCALE = HEAD_DIM ** -0.5

V7X_LANES = 128
SUBLANES = 8
V7X_MXU_DIM = 256
V7X_VMEM_LIMIT = 56 * 1024 * 1024
ROW_TILE = 512
FF_CHUNK = 512
CONV_TILE = 512
HALO_ROWS = 32
CONV_ROWS = 64
PAIRS_PER_STEP = 4
BAND_Q = 256
BAND_WIN = BAND_PAST + BAND_Q
BAND_BLOCKS_PER_STEP = 2
REL_FIRST = REL_CLIP + 1 - CHUNK
REL_COLS = 256
BIAS_UNROLL = 8


def _params(*semantics):
    return pltpu.CompilerParams(dimension_semantics=semantics, vmem_limit_bytes=V7X_VMEM_LIMIT)


def _dot(a, b):
    return jnp.dot(a, b, preferred_element_type=F32)


def _dot_t(a, b):
    return lax.dot_general(a, b, (((1,), (1,)), ((), ())), preferred_element_type=F32)


def _rms_rows(x, g):
    r = lax.rsqrt(jnp.mean(x * x, axis=-1, keepdims=True) + RMS_EPS)
    return (x * r) * g


def _resident(shape):
    return pl.BlockSpec(shape, lambda *_: (0,) * len(shape), pipeline_mode=pl.Buffered(1))


def _lane_lo():
    return lax.broadcasted_iota(jnp.int32, (1, V7X_LANES), 1) < HEAD_DIM


def _row_lo():
    return lax.broadcasted_iota(jnp.int32, (V7X_LANES, 1), 0) < HEAD_DIM


def _split_heads(xp, lane_lo):
    zero = jnp.zeros_like(xp)
    return jnp.concatenate([jnp.where(lane_lo, xp, zero), jnp.where(lane_lo, zero, xp)], axis=0)


def _split_heads_t(xt, row_lo):
    zero = jnp.zeros_like(xt)
    return jnp.concatenate([jnp.where(row_lo, xt, zero), jnp.where(row_lo, zero, xt)], axis=1)


def _pair(p):
    return slice(p * V7X_LANES, (p + 1) * V7X_LANES)


def _store_heads(dst_ref, src_ref):
    nh = dst_ref.shape[1]
    per_head = jnp.stack([src_ref[:, h * HEAD_DIM:(h + 1) * HEAD_DIM] for h in range(nh)], axis=0)
    dst_ref[...] = jnp.swapaxes(per_head, 0, 1)


def _ab_in_kernel(x_ref, g_ref, w_ref, q_ref, kb_ref, vb_ref, k_ref, v_ref, u_ref, h_ref, kf_ref, vf_ref):
    h_ref[...] = _rms_rows(x_ref[...], g_ref[...]).astype(BF16)
    sw = q_ref.shape[-1]
    cw = u_ref.shape[-1]
    proj = lambda c0, n: _dot(h_ref[...], w_ref[:, c0:c0 + n])
    q_ref[...] = (proj(0, sw) * SCALE).astype(BF16)
    for c0, b_ref, o_ref, f_ref in ((sw, kb_ref, k_ref, kf_ref), (2 * sw, vb_ref, v_ref, vf_ref)):
        f_ref[...] = proj(c0, sw)
        b_ref[...] = f_ref[...].astype(BF16)
        _store_heads(o_ref, f_ref)
    u_ref[...] = proj(3 * sw, cw) * jax.nn.sigmoid(proj(3 * sw + cw, cw))


def _ab_in(x, g, w, sw, cw, tm):
    m, d = x.shape
    nh = sw // HEAD_DIM
    row = lambda n: pl.BlockSpec((tm, n), lambda i: (i, 0))
    heads = pl.BlockSpec((tm, nh, HEAD_DIM), lambda i: (i, 0, 0))
    return pl.pallas_call(
        _ab_in_kernel,
        grid=(m // tm,),
        in_specs=[row(d), _resident((1, d)), _resident(w.shape)],
        out_specs=[row(sw)] * 3 + [heads] * 2 + [row(cw)],
        out_shape=[jax.ShapeDtypeStruct((m, sw), BF16)] * 3 + [jax.ShapeDtypeStruct((m, nh, HEAD_DIM), F32)] * 2
        + [jax.ShapeDtypeStruct((m, cw), F32)],
        scratch_shapes=[pltpu.VMEM((tm, d), BF16)] + [pltpu.VMEM((tm, sw), F32)] * 2,
        compiler_params=_params("arbitrary"),
        name="ab_in_proj",
    )(x, g, w)


def _ab_in_t_kernel(x_ref, g_ref, w_ref, wkv_ref, q_ref, ktb_ref, vtb_ref, kt_ref, vt_ref, u_ref, h_ref):
    h_ref[...] = _rms_rows(x_ref[...], g_ref[...]).astype(BF16)
    sw = q_ref.shape[-1]
    cw = u_ref.shape[-1]
    proj = lambda c0, n: _dot(h_ref[...], w_ref[:, c0:c0 + n])
    q_ref[...] = (proj(0, sw) * SCALE).astype(BF16)
    for r0, b_ref, f_ref in ((0, ktb_ref, kt_ref), (sw, vtb_ref, vt_ref)):
        f_ref[0] = _dot_t(wkv_ref[r0:r0 + sw, :], h_ref[...])
        b_ref[0] = f_ref[0].astype(BF16)
    u_ref[...] = proj(3 * sw, cw) * jax.nn.sigmoid(proj(3 * sw + cw, cw))


def _ab_in_t(x, g, w, wkv_t, sw, cw, t, tm):
    m, d = x.shape
    per = t // tm
    row = lambda n: pl.BlockSpec((tm, n), lambda i: (i, 0))
    tr = pl.BlockSpec((1, sw, tm), lambda i: (i // per, 0, i % per))
    tshape = lambda dt: jax.ShapeDtypeStruct((m // t, sw, t), dt)
    return pl.pallas_call(
        _ab_in_t_kernel,
        grid=(m // tm,),
        in_specs=[row(d), _resident((1, d)), _resident(w.shape), _resident(wkv_t.shape)],
        out_specs=[row(sw), tr, tr, tr, tr, row(cw)],
        out_shape=[jax.ShapeDtypeStruct((m, sw), BF16), tshape(BF16), tshape(BF16), tshape(F32), tshape(F32),
                   jax.ShapeDtypeStruct((m, cw), F32)],
        scratch_shapes=[pltpu.VMEM((tm, d), BF16)],
        compiler_params=_params("arbitrary"),
        name="ab_in_proj_t",
    )(x, g, w, wkv_t)


def _softplus(z):
    return jnp.maximum(z, 0.0) + jnp.log(1.0 + jnp.exp2(jnp.abs(z) * -LOG2E))


def _sb_scores(q2s, ktblks):
    return [_dot(q2, ktblk) for q2, ktblk in zip(q2s, ktblks)]


def _sb_cumsums(zs, tri2, mask):
    cs = []
    for z in zs:
        sp = _softplus(z)
        if mask is not None:
            sp = jnp.where(mask, sp, 0.0)
        hi = sp.astype(BF16)
        lo = (sp - hi.astype(F32)).astype(BF16)
        cs.append(_dot(jnp.concatenate([hi, lo], axis=1), tri2))
    return cs


def _sb_weigh(zs, cs, vtblks, states, mask, row_lo):
    out = []
    for z, c, vtblk, (carry, acc) in zip(zs, cs, vtblks, states):
        bq = acc.shape[0]
        w = jnp.exp2((z - c - carry) * LOG2E)
        if mask is not None:
            w = jnp.where(mask, w, 0.0)
        w = w.astype(BF16)
        wcat = jnp.concatenate([w[:bq], w[bq:]], axis=1)
        out.append((carry + c[:, 0:1], acc + _dot_t(wcat, _split_heads_t(vtblk, row_lo))))
    return tuple(out)


def _sb_blocks(q2s, ktblks, vtblks, tri2, states, mask, row_lo):
    zs = _sb_scores(q2s, ktblks)
    return _sb_weigh(zs, _sb_cumsums(zs, tri2, mask), vtblks, states, mask, row_lo)


def _sb_consts(bq, kb):
    tri = jnp.where((lax.broadcasted_iota(jnp.int32, (2 * kb, kb), 0) & (kb - 1))
                    >= lax.broadcasted_iota(jnp.int32, (2 * kb, kb), 1), 1.0, 0.0).astype(BF16)
    assert bq & (bq - 1) == 0
    t_loc = lax.broadcasted_iota(jnp.int32, (2 * bq, kb), 0) & (bq - 1)
    causal = lax.broadcasted_iota(jnp.int32, (2 * bq, kb), 1) < t_loc
    return _lane_lo(), tri, causal


def _sb_zero_state(bq):
    return jnp.zeros((2 * bq, 1), F32), jnp.zeros((bq, V7X_LANES), F32)


def _sb_prompt_kernel(q_ref, kt_ref, vt_ref, o_ref):
    i = pl.program_id(2)
    bq = q_ref.shape[1]
    npair = q_ref.shape[2] // V7X_LANES
    lane_lo, tri, causal = _sb_consts(bq, bq)
    row_lo = _row_lo()
    q2 = [_split_heads(q_ref[0, :, _pair(p)], lane_lo) for p in range(npair)]

    def blocks(j):
        cols = pl.ds(pl.multiple_of(j * bq, bq), bq)
        return ([kt_ref[0, _pair(p), cols] for p in range(npair)], [vt_ref[0, _pair(p), cols] for p in range(npair)])

    def sweep(j, states, mask):
        return _sb_blocks(q2, *blocks(j), tri, states, mask, row_lo)

    def sweep_many(js, states):
        kvs = [blocks(j) for j in js]
        zs = [_sb_scores(q2, kt) for kt, _ in kvs]
        cs = [_sb_cumsums(z, tri, None) for z in zs]
        for z, c, (_, vt) in zip(zs, cs, kvs):
            states = _sb_weigh(z, c, vt, states, None, row_lo)
        return states

    states = sweep(i, (_sb_zero_state(bq),) * npair, causal)
    states = lax.fori_loop(0, i // 3, lambda s, st: sweep_many([i - 1 - 3 * s - k for k in range(3)], st), states)
    rem = i % 3
    states = lax.cond(rem == 2, lambda st: sweep_many([1, 0], st), lambda st: st, states)
    states = lax.cond(rem == 1, lambda st: sweep(0, st, None), lambda st: st, states)
    for p in range(npair):
        o_ref[0, :, _pair(p)] = states[p][1].astype(BF16)


def _sb_prompt(q, kt, vt):
    b, t, sw = q.shape
    bq = V7X_MXU_DIM
    width = PAIRS_PER_STEP * V7X_LANES
    qspec = pl.BlockSpec((1, bq, width), lambda bi, p, i: (bi, i, p))
    kvspec = pl.BlockSpec((1, width, t), lambda bi, p, i: (bi, p, 0))
    return pl.pallas_call(
        _sb_prompt_kernel,
        grid=(b, sw // width, t // bq),
        in_specs=[qspec, kvspec, kvspec],
        out_specs=qspec,
        out_shape=jax.ShapeDtypeStruct((b, t, sw), BF16),
        compiler_params=_params("arbitrary", "arbitrary", "arbitrary"),
        name="sb_attn_prompt",
    )(q, kt, vt)


def _pair_rows(ref, p):
    return jnp.concatenate([ref[0, 2 * p], ref[0, 2 * p + 1]], axis=0).astype(BF16)


def _sb_sample_kernel(q_ref, kc_ref, vc_ref, kn_ref, vn_ref, o_ref, kpad_ref, vpad_ref):
    bq = q_ref.shape[1]
    kb = kpad_ref.shape[0]
    npair = q_ref.shape[2] // V7X_LANES
    nblk = kc_ref.shape[3] // kb
    lane_lo, tri2, causal = _sb_consts(bq, kb)
    row_lo = _row_lo()
    kpad_ref[...] = jnp.zeros_like(kpad_ref)
    vpad_ref[...] = jnp.zeros_like(vpad_ref)
    kpad_ref[0:bq, :] = kn_ref[0]
    vpad_ref[0:bq, :] = vn_ref[0]
    cols = [slice(j * kb, (j + 1) * kb) for j in reversed(range(nblk))]
    for p in range(npair):
        q2 = _split_heads(q_ref[0, :, _pair(p)], lane_lo)
        kt, vt = _pair_rows(kc_ref, p), _pair_rows(vc_ref, p)
        z = [_dot_t(q2, kpad_ref[:, _pair(p)])] + [_dot(q2, kt[:, c]) for c in cols]
        sp = [_softplus(zb) for zb in z]
        sp[0] = jnp.where(causal, sp[0], 0.0)
        sp_all = jnp.concatenate(sp, axis=0)
        hi = sp_all.astype(BF16)
        lo = (sp_all - hi.astype(F32)).astype(BF16)
        c_all = _dot(jnp.concatenate([hi, lo], axis=1), tri2)
        carry = jnp.zeros((2 * bq, 1), F32)
        ws = []
        for j, zb in enumerate(z):
            c = c_all[j * 2 * bq:(j + 1) * 2 * bq]
            w = jnp.exp2((zb - c - carry) * LOG2E)
            if j == 0:
                w = jnp.where(causal, w, 0.0)
            carry = carry + c[:, 0:1]
            w = w.astype(BF16)
            ws.append(jnp.concatenate([w[:bq], w[bq:]], axis=1))
        vt_cat = jnp.concatenate([_split_heads_t(vt[:, c], row_lo) for c in cols], axis=1)
        acc = _dot(ws[0], _split_heads(vpad_ref[:, _pair(p)], lane_lo)) + _dot_t(jnp.concatenate(ws[1:], axis=1), vt_cat)
        o_ref[0, :, _pair(p)] = acc.astype(BF16)


def _sb_sample(q, kc, vc, kn, vn):
    b, t, sw = q.shape
    new = pl.BlockSpec((1, t, sw), lambda bi: (bi, 0, 0))
    cache = pl.BlockSpec((1,) + kc.shape[1:], lambda bi: (bi, 0, 0, 0))
    return pl.pallas_call(
        _sb_sample_kernel,
        grid=(b,),
        in_specs=[new, cache, cache, new, new],
        out_specs=new,
        out_shape=jax.ShapeDtypeStruct((b, t, sw), BF16),
        scratch_shapes=[pltpu.VMEM((V7X_MXU_DIM, sw), BF16)] * 2,
        compiler_params=_params("arbitrary"),
        name="sb_attn_sample",
    )(q, kc, vc, kn, vn)


def _conv_kernel(u_ref, halo_ref, dw_ref, b_ref, g_ref, be_ref, c_ref, ext_ref, sh_ref, *, halo_is_history):
    tm = u_ref.shape[1]
    halo = halo_ref[0]
    if not halo_is_history:
        halo = jnp.where(pl.program_id(1) > 0, halo, 0.0)
    ext_ref[0:HALO_ROWS, :] = halo
    ext_ref[HALO_ROWS:, :] = u_ref[0]
    n = sh_ref.shape[1]
    for s in range(1, SUBLANES):
        sh_ref[s - 1] = ext_ref[s:s + n, :]
    first = HALO_ROWS - CONV_STATE
    rows = min(CONV_ROWS, tm)
    for r0 in range(0, tm, rows):
        y = b_ref[...]
        for w in range(CONV_WIDTH):
            s = (first + w) % SUBLANES
            a = r0 + first + w - s
            tap = ext_ref[a:a + rows, :] if s == 0 else sh_ref[s - 1, a:a + rows, :]
            y = y + tap * dw_ref[w:w + 1, :]
        mu = jnp.mean(y, axis=-1, keepdims=True)
        var = jnp.mean(jnp.square(y - mu), axis=-1, keepdims=True)
        yn = ((y - mu) * lax.rsqrt(var + LN_EPS)) * g_ref[...] + be_ref[...]
        c_ref[0, r0:r0 + rows, :] = (yn * jax.nn.sigmoid(yn)).astype(BF16)


def _conv(u, halo_src, dw, b, g, be, tm, halo_is_history):
    bsz, t, c = u.shape
    if halo_is_history:
        halo_spec = pl.BlockSpec((1, HALO_ROWS, c), lambda bi, ti: (bi, 0, 0))
    else:
        per = tm // HALO_ROWS
        halo_spec = pl.BlockSpec((1, HALO_ROWS, c), lambda bi, ti: (bi, jnp.maximum(ti * per - 1, 0), 0))
    tile = pl.BlockSpec((1, tm, c), lambda bi, ti: (bi, ti, 0))
    vec = _resident((1, c))
    return pl.pallas_call(
        functools.partial(_conv_kernel, halo_is_history=halo_is_history),
        grid=(bsz, t // tm),
        in_specs=[tile, halo_spec, _resident(dw.shape), vec, vec, vec],
        out_specs=tile,
        out_shape=jax.ShapeDtypeStruct((bsz, t, c), BF16),
        scratch_shapes=[pltpu.VMEM((HALO_ROWS + tm, c), F32),
                        pltpu.VMEM((SUBLANES - 1, HALO_ROWS + tm - SUBLANES, c), F32)],
        compiler_params=_params("arbitrary", "arbitrary"),
        name="conv_module",
    )(u, halo_src, dw, b, g, be)


def _out_mlp_kernel(x_ref, m0_ref, m1_ref, wo_ref, g_ref, wu_ref, wd_ref, gf_ref, y_ref, h_ref, *, final_norm):
    half = m0_ref.shape[-1]
    x1 = x_ref[...] + _dot(m0_ref[...], wo_ref[0:half, :]) + _dot(m1_ref[...], wo_ref[half:, :])
    h_ref[...] = _rms_rows(x1, g_ref[...]).astype(BF16)
    y_ref[...] = x1
    for c0 in range(0, wu_ref.shape[1], FF_CHUNK):
        up = _dot(h_ref[...], wu_ref[:, c0:c0 + FF_CHUNK])
        act = jnp.square(jnp.maximum(up, 0.0)).astype(BF16)
        y_ref[...] += _dot(act, wd_ref[c0:c0 + FF_CHUNK, :])
    if final_norm:
        y_ref[...] = _rms_rows(y_ref[...], gf_ref[...])


def _out_mlp(x, m0, m1, m1_block, wo, g, wu, wd, gf, tm, final_norm):
    m, d = x.shape
    half = wo.shape[0] // 2
    row = pl.BlockSpec((tm, d), lambda i: (i, 0))
    return pl.pallas_call(
        functools.partial(_out_mlp_kernel, final_norm=final_norm),
        grid=(m // tm,),
        in_specs=[row, pl.BlockSpec((tm, half), lambda i: (i, 0)), pl.BlockSpec((tm, half), lambda i: (i, m1_block)),
                  _resident(wo.shape), _resident((1, d)), _resident(wu.shape), _resident(wd.shape), _resident((1, d))],
        out_specs=row,
        out_shape=jax.ShapeDtypeStruct((m, d), F32),
        scratch_shapes=[pltpu.VMEM((tm, d), BF16)],
        compiler_params=_params("arbitrary"),
        name="out_proj_mlp",
    )(x, m0, m1, wo, g, wu, wd, gf)


def _c_in_kernel(x_ref, g_ref, w_ref, q_ref, k_ref, v_ref, kt_ref, vt_ref, h_ref, kf_ref, vf_ref, *, tail_every):
    h_ref[...] = _rms_rows(x_ref[...], g_ref[...]).astype(BF16)
    cw = q_ref.shape[-1]
    half = cw // 2
    for c0 in range(0, cw, half):
        cols = slice(c0, c0 + half)
        q_ref[:, cols] = (_dot(h_ref[...], w_ref[:, c0:c0 + half]) * SCALE).astype(BF16)
        for base, b_ref, f_ref in ((cw, k_ref, kf_ref), (2 * cw, v_ref, vf_ref)):
            f_ref[:, cols] = _dot(h_ref[...], w_ref[:, base + c0:base + c0 + half])
            b_ref[:, cols] = f_ref[:, cols].astype(BF16)

    @pl.when((pl.program_id(0) + 1) % tail_every == 0)
    def _():
        _store_heads(kt_ref, kf_ref)
        _store_heads(vt_ref, vf_ref)


def _c_in(x, g, w, t, keep, tm):
    m, d = x.shape
    cw = w.shape[1] // 3
    nh = cw // HEAD_DIM
    row = lambda n: pl.BlockSpec((tm, n), lambda i: (i, 0))
    if keep == t:
        tail_every = 1
        tail_map = lambda i: (i, 0, 0)
    else:
        assert keep == tm and t % tm == 0
        tail_every = t // tm
        tail_map = lambda i: (i // tail_every, 0, 0)
    tail = pl.BlockSpec((tm, nh, HEAD_DIM), tail_map)
    nb = m // t
    return pl.pallas_call(
        functools.partial(_c_in_kernel, tail_every=tail_every),
        grid=(m // tm,),
        in_specs=[row(d), _resident((1, d)), _resident(w.shape)],
        out_specs=[row(cw), row(cw), row(cw), tail, tail],
        out_shape=[jax.ShapeDtypeStruct((m, cw), BF16)] * 3 + [jax.ShapeDtypeStruct((nb * keep, nh, HEAD_DIM), F32)] * 2,
        scratch_shapes=[pltpu.VMEM((tm, d), BF16)] + [pltpu.VMEM((tm, cw), F32)] * 2,
        compiler_params=_params("arbitrary"),
        name="c_in_proj",
    )(x, g, w)


def _c_in_t_kernel(x_ref, g_ref, w_ref, wkv_ref, q_ref, ktb_ref, vtb_ref, kt_ref, vt_ref, h_ref):
    h_ref[...] = _rms_rows(x_ref[...], g_ref[...]).astype(BF16)
    cw = q_ref.shape[-1]
    half = cw // 2
    for c0 in range(0, cw, half):
        q_ref[:, c0:c0 + half] = (_dot(h_ref[...], w_ref[:, c0:c0 + half]) * SCALE).astype(BF16)
    for base, b_ref, f_ref in ((0, ktb_ref, kt_ref), (cw, vtb_ref, vt_ref)):
        for r0 in range(0, cw, half):
            f_ref[0, r0:r0 + half, :] = _dot_t(wkv_ref[base + r0:base + r0 + half, :], h_ref[...])
            b_ref[0, r0:r0 + half, :] = f_ref[0, r0:r0 + half, :].astype(BF16)


def _c_in_t(x, g, w, wkv_t, t, keep, tm):
    m, d = x.shape
    cw = w.shape[1]
    assert keep == tm and t % tm == 0
    per = t // tm
    nb = m // t
    row = lambda n: pl.BlockSpec((tm, n), lambda i: (i, 0))
    tr = pl.BlockSpec((1, cw, tm), lambda i: (i // per, 0, i % per))
    tail = pl.BlockSpec((1, cw, keep), lambda i: (i // per, 0, 0))
    return pl.pallas_call(
        _c_in_t_kernel,
        grid=(m // tm,),
        in_specs=[row(d), _resident((1, d)), _resident(w.shape), _resident(wkv_t.shape)],
        out_specs=[row(cw), tr, tr, tail, tail],
        out_shape=[jax.ShapeDtypeStruct((m, cw), BF16)] + [jax.ShapeDtypeStruct((nb, cw, t), BF16)] * 2
        + [jax.ShapeDtypeStruct((nb, cw, keep), F32)] * 2,
        scratch_shapes=[pltpu.VMEM((tm, d), BF16)],
        compiler_params=_params("arbitrary"),
        name="c_in_proj_t",
    )(x, g, w, wkv_t)


def _bias_kernel(rb_ref, o_ref):
    rb = rb_ref[...]
    nh = rb.shape[0]
    hi = rb.astype(BF16)
    r1 = rb - hi.astype(F32)
    mid = r1.astype(BF16)
    lo = (r1 - mid.astype(F32)).astype(BF16)
    terms = jnp.concatenate([hi, mid, lo], axis=0)
    ncol, win = rb.shape[1], o_ref.shape[2]
    s = lax.broadcasted_iota(jnp.int32, (1, win), 1)
    kk = lax.broadcasted_iota(jnp.int32, (ncol, win), 0)

    def body(r, carry):
        idx = jnp.clip(BAND_PAST + r - s, -REL_CLIP, REL_CLIP) + REL_CLIP - REL_FIRST
        kc, qc = s >> CHUNK_SHIFT, r >> CHUNK_SHIFT
        visible = (kc >= qc) & (kc <= qc + LEFT_CHUNKS)
        onehot = jnp.where((kk == idx) & visible, 1.0, 0.0).astype(BF16)
        parts = _dot(terms, onehot)
        row = (parts[0:nh] + parts[nh:2 * nh]) + parts[2 * nh:3 * nh]
        o_ref[r] = jnp.where(visible, row, -jnp.inf)
        return carry

    lax.fori_loop(0, o_ref.shape[0], body, 0, unroll=BIAS_UNROLL)


def _bias_tile(rel_bias):
    h = rel_bias.shape[0]
    cols = rel_bias[:, REL_FIRST:]
    cols = jnp.pad(cols, ((0, 0), (0, REL_COLS - cols.shape[1])))
    tile = pl.pallas_call(
        _bias_kernel,
        out_shape=jax.ShapeDtypeStruct((BAND_Q, h, BAND_WIN), F32),
        compiler_params=pltpu.CompilerParams(vmem_limit_bytes=V7X_VMEM_LIMIT),
        name="band_bias_tile",
    )(cols)
    return jnp.transpose(tile, (1, 0, 2))


def _band_core(q_pairs, kwts, vwts, biases, valid, lane_lo, row_lo):
    bq = q_pairs[0].shape[0]
    zs = [_dot(_split_heads(q_pair, lane_lo), kwt) for q_pair, kwt in zip(q_pairs, kwts)]
    es, invs = [], []
    for z, (bias0, bias1), ok in zip(zs, biases, valid):
        s = z + jnp.concatenate([bias0, bias1], axis=0)
        if ok is not None:
            s = jnp.where(ok, s, -jnp.inf)
        e = jnp.exp2((s - jnp.max(s, axis=-1, keepdims=True)) * LOG2E)
        invs.append(1.0 / jnp.sum(e, axis=-1, keepdims=True))
        es.append(e.astype(BF16))
    outs = []
    for e, inv, vwt in zip(es, invs, vwts):
        o = _dot_t(jnp.concatenate([e[:bq], e[bq:]], axis=1), _split_heads_t(vwt, row_lo))
        outs.append((o * jnp.where(lane_lo, inv[:bq], inv[bq:])).astype(BF16))
    return outs


def _band_prompt_kernel(q_ref, kt_ref, vt_ref, bias_ref, o_ref, kpad_ref, vpad_ref):
    i = pl.program_id(2)
    bq, win = bias_ref.shape[1], bias_ref.shape[2]
    nblk = q_ref.shape[1] // bq
    npair = q_ref.shape[2] // V7X_LANES
    past = win - bq

    @pl.when(i == 0)
    def _():
        kpad_ref[:, 0:past] = jnp.zeros((kpad_ref.shape[0], past), BF16)
        vpad_ref[:, 0:past] = jnp.zeros((vpad_ref.shape[0], past), BF16)
        kpad_ref[:, past:] = kt_ref[0]
        vpad_ref[:, past:] = vt_ref[0]

    col = lax.broadcasted_iota(jnp.int32, (1, win), 1)
    work = [(j, p) for j in range(nblk) for p in range(npair)]
    cols = [pl.ds(pl.multiple_of((i * nblk + j) * bq, bq), win) for j in range(nblk)]

    def run(masked):
        valid = [col >= past - (i * nblk + j) * bq if masked else None for j in range(nblk)]
        outs = _band_core([q_ref[0, j * bq:(j + 1) * bq, _pair(p)] for j, p in work],
                          [kpad_ref[_pair(p), cols[j]] for j, p in work], [vpad_ref[_pair(p), cols[j]] for j, p in work],
                          [(bias_ref[2 * p], bias_ref[2 * p + 1]) for j, p in work], [valid[j] for j, p in work],
                          _lane_lo(), _row_lo())
        for (j, p), out in zip(work, outs):
            o_ref[0, j * bq:(j + 1) * bq, _pair(p)] = out

    lax.cond(i * nblk * bq < past, lambda: run(True), lambda: run(False))


def _band_prompt(q, kt, vt, bias):
    b, t, cw = q.shape
    bq, win = bias.shape[1], bias.shape[2]
    width = PAIRS_PER_STEP * V7X_LANES
    rows = BAND_BLOCKS_PER_STEP * bq
    qspec = pl.BlockSpec((1, rows, width), lambda p, bi, i: (bi, i, p))
    kvspec = pl.BlockSpec((1, width, t), lambda p, bi, i: (bi, p, 0))
    return pl.pallas_call(
        _band_prompt_kernel,
        grid=(cw // width, b, t // rows),
        in_specs=[qspec, kvspec, kvspec, pl.BlockSpec((2 * PAIRS_PER_STEP, bq, win), lambda p, bi, i: (p, 0, 0))],
        out_specs=qspec,
        out_shape=jax.ShapeDtypeStruct((b, t, cw), BF16),
        scratch_shapes=[pltpu.VMEM((width, win - bq + t), BF16)] * 2,
        compiler_params=_params("arbitrary", "arbitrary", "arbitrary"),
        name="band_attn_prompt",
    )(q, kt, vt, bias)


def _band_sample_kernel(q_ref, kc_ref, vc_ref, kn_ref, vn_ref, bias_ref, o_ref, kpad_ref, vpad_ref):
    t = q_ref.shape[1]
    npair = q_ref.shape[2] // V7X_LANES
    past = kc_ref.shape[3]
    win = bias_ref.shape[2]
    kpad_ref[...] = jnp.zeros_like(kpad_ref)
    vpad_ref[...] = jnp.zeros_like(vpad_ref)
    kpad_ref[0:t, :] = kn_ref[0]
    vpad_ref[0:t, :] = vn_ref[0]
    valid = lax.broadcasted_iota(jnp.int32, (1, win), 1) < past + t
    lane_lo, row_lo = _lane_lo(), _row_lo()
    pairs = range(npair)
    q2s = [_split_heads(q_ref[0, :, _pair(p)], lane_lo) for p in pairs]
    zs = [jnp.concatenate([_dot(q2s[p], _pair_rows(kc_ref, p)), _dot_t(q2s[p], kpad_ref[:, _pair(p)])], axis=1)
          for p in pairs]
    for p in pairs:
        s = zs[p] + jnp.concatenate([bias_ref[2 * p], bias_ref[2 * p + 1]], axis=0)
        s = jnp.where(valid, s, -jnp.inf)
        e = jnp.exp2((s - jnp.max(s, axis=-1, keepdims=True)) * LOG2E)
        inv = 1.0 / jnp.sum(e, axis=-1, keepdims=True)
        e = e.astype(BF16)
        e_cache = jnp.concatenate([e[:t, :past], e[t:, :past]], axis=1)
        e_new = jnp.concatenate([e[:t, past:], e[t:, past:]], axis=1)
        o = (_dot_t(e_cache, _split_heads_t(_pair_rows(vc_ref, p), row_lo))
             + _dot(e_new, _split_heads(vpad_ref[:, _pair(p)], lane_lo)))
        o_ref[0, :, _pair(p)] = (o * jnp.where(lane_lo, inv[:t], inv[t:])).astype(BF16)


def _band_sample(q, kc, vc, kn, vn, bias):
    b, t, cw = q.shape
    nh = bias.shape[0]
    past, win = kc.shape[3], bias.shape[2]
    new = pl.BlockSpec((1, t, cw), lambda bi: (bi, 0, 0))
    cache = pl.BlockSpec((1,) + kc.shape[1:], lambda bi: (bi, 0, 0, 0))
    return pl.pallas_call(
        _band_sample_kernel,
        grid=(b,),
        in_specs=[new, cache, cache, new, new, pl.BlockSpec((nh, t, win), lambda bi: (0, 0, 0))],
        out_specs=new,
        out_shape=jax.ShapeDtypeStruct((b, t, cw), BF16),
        scratch_shapes=[pltpu.VMEM((win - past, cw), BF16)] * 2,
        compiler_params=_params("arbitrary"),
        name="band_attn_sample",
    )(q, kc, vc, kn, vn, bias)


def kernel(x_prompt, x_sample, cache_sb_k, cache_sb_v, cache_conv, cache_band_k, cache_band_v, norm_mix, norm_ffn,
           norm_final, w_in_ab, w_out_ab, dw_w, dw_b, conv_ln_g, conv_ln_b, w_in_c, w_out_c, rel_bias, w_up, w_down):
    b, t, d = x_prompt.shape
    bs, ts, _ = x_sample.shape
    n_sb, n_c = cache_sb_k.shape[3], cache_band_k.shape[3]
    sw, cc, cw = n_sb * HEAD_DIM, dw_w.shape[2], n_c * HEAD_DIM
    past = cache_sb_k.shape[2]
    band_past = cache_band_k.shape[2]
    keep = min(BAND_PAST, t)
    assert w_in_ab.shape[0] == 1 and w_in_c.shape[0] == 1 and norm_mix.shape[0] == 2
    assert sw == cc and sw + cc == d and cw == d and w_in_ab.shape[2] == 3 * sw + 2 * cc
    assert t % ROW_TILE == 0 and ts % 16 == 0 and ts <= V7X_MXU_DIM
    assert past % V7X_MXU_DIM == 0 and band_past == BAND_PAST and band_past + ts <= BAND_WIN

    tm_p, tm_s = ROW_TILE, bs * ts
    vec = lambda a: a.reshape(1, -1)
    w_in0, w_out0 = w_in_ab[0].astype(BF16), w_out_ab[0].astype(BF16)
    w_in1, w_out1 = w_in_c[0].astype(BF16), w_out_c[0].astype(BF16)
    w_up_b, w_down_b = w_up.astype(BF16), w_down.astype(BF16)
    dw_pad = jnp.pad(dw_w[0], ((0, HALO_ROWS - CONV_WIDTH), (0, 0)))
    conv_vecs = (vec(dw_b[0]), vec(conv_ln_g[0]), vec(conv_ln_b[0]))
    bias = _bias_tile(rel_bias[0])

    shp = lambda z, n: z.reshape(b, t, n)
    x0 = x_prompt.reshape(b * t, d)
    wkv0_t = jnp.transpose(w_in0[:, sw:3 * sw])
    q, ktb, vtb, kp, vp, up = _ab_in_t(x0, vec(norm_mix[0]), w_in0, wkv0_t, sw, cc, t, tm_p)
    a = _sb_prompt(shp(q, sw), ktb, vtb).reshape(b * t, sw)
    c = _conv(shp(up, cc), shp(up, cc), dw_pad, *conv_vecs, CONV_TILE, False).reshape(b * t, cc)
    xp2 = _out_mlp(x0, a, c, 0, w_out0, vec(norm_ffn[0]), w_up_b[0], w_down_b[0], vec(norm_final), tm_p, False)
    wkv1_t = jnp.transpose(w_in1[:, cw:])
    q, ktb, vtb, bkp, bvp = _c_in_t(xp2, vec(norm_mix[1]), w_in1[:, :cw], wkv1_t, t, keep, tm_p)
    o = _band_prompt(shp(q, cw), ktb, vtb, bias).reshape(b * t, cw)
    yp = _out_mlp(xp2, o, o, 1, w_out1, vec(norm_ffn[1]), w_up_b[1], w_down_b[1], vec(norm_final), tm_p, True)
    by_time = lambda zt, nh: jnp.transpose(zt.reshape(b, nh, HEAD_DIM, -1), (0, 3, 1, 2))[None]

    def layer0(x, tm, attn, conv):
        q, kb, vb, k, v, u = _ab_in(x, vec(norm_mix[0]), w_in0, sw, cc, tm)
        a = attn(q, kb, vb)
        x2 = _out_mlp(x, a, conv(u), 0, w_out0, vec(norm_ffn[0]), w_up_b[0], w_down_b[0], vec(norm_final), tm, False)
        return x2, k, v, u

    def layer1(x, tm, tlen, tail, attn):
        q, k, v, kt, vt = _c_in(x, vec(norm_mix[1]), w_in1, tlen, tail, tm)
        o = attn(q, k, v)
        y = _out_mlp(x, o, o, 1, w_out1, vec(norm_ffn[1]), w_up_b[1], w_down_b[1], vec(norm_final), tm, True)
        return y, kt, vt

    shs = lambda z, n: z.reshape(bs, ts, n)
    conv_hist = jnp.pad(cache_conv[0], ((0, 0), (HALO_ROWS - CONV_STATE, 0), (0, 0)))
    by_head = lambda c: jnp.transpose(c[0], (0, 2, 3, 1))
    ck, cv = by_head(cache_sb_k), by_head(cache_sb_v)
    xs2, ks, vs, us = layer0(
        x_sample.reshape(bs * ts, d), tm_s,
        lambda q, k, v: _sb_sample(shs(q, sw), ck, cv, shs(k, sw), shs(v, sw)).reshape(bs * ts, sw),
        lambda u: _conv(shs(u, cc), conv_hist, dw_pad, *conv_vecs, ts, True).reshape(bs * ts, cc))
    bck, bcv = by_head(cache_band_k), by_head(cache_band_v)
    ys, bks, bvs = layer1(
        xs2, tm_s, ts, ts,
        lambda q, k, v: _band_sample(shs(q, cw), bck, bcv, shs(k, cw), shs(v, cw), bias).reshape(bs * ts, cw))

    new_conv_p = up.reshape(b, t, cc)[:, t - CONV_STATE:]
    new_conv_s = jnp.concatenate([cache_conv[0], us.reshape(bs, ts, cc)], axis=1)[:, ts:]
    return (yp.reshape(b, t, d), ys.reshape(bs, ts, d),
            by_time(kp, n_sb), by_time(vp, n_sb), new_conv_p[None], by_time(bkp, n_c), by_time(bvp, n_c),
            ks.reshape(1, bs, ts, n_sb, HEAD_DIM), vs.reshape(1, bs, ts, n_sb, HEAD_DIM), new_conv_s[None],
            bks.reshape(1, bs, ts, n_c, HEAD_DIM), bvs.reshape(1, bs, ts, n_c, HEAD_DIM))
```
